```python
import math
import jax, jax.numpy as jnp
from jax import lax
import numpy as np

D_MODEL = 1024
BATCH = 8
SEQ = 2048
DEPTH = 4

CONV_WIDTH = D_MODEL // 2
CONV_HEAD_DIM = 64
N_CONV_HEADS = CONV_WIDTH // CONV_HEAD_DIM
CONV_K = 3
SSM_WIDTH = D_MODEL - CONV_WIDTH
SSM_GROUP = 16
SSM_GROUPS = SSM_WIDTH // SSM_GROUP
SSM_STATE = 64
MIX_WIDTH = CONV_WIDTH + SSM_WIDTH
IN_COLS = 3 * CONV_WIDTH + SSM_WIDTH
D_FF = ((8 * D_MODEL // 3 + 127) // 128) * 128
PLE_DIM = 256
EPS = 1e-6
DT_MIN = 1e-3
DT_MAX = 1e-1

kernel_name = "hymba_style_conv_s5_macaron_trunk"


def rmsnorm(x, g):
    xf = x.astype(jnp.float32)
    xf = xf * lax.rsqrt(jnp.mean(xf * xf, axis=-1, keepdims=True) + EPS)
    return (xf * g.astype(jnp.float32)).astype(x.dtype)


def swiglu(u, w_gate, w_up, w_down):
    return (jax.nn.silu(u @ w_gate) * (u @ w_up)) @ w_down


def short_conv_group(z_b, z_c, z_v, conv_w, conv_b):
    v = z_c * z_v
    rhs = conv_w.astype(v.dtype)[:, None, :]
    y = lax.conv_general_dilated(
        v, rhs, window_strides=(1,), padding=[(CONV_K - 1, 0)],
        dimension_numbers=("NWC", "WIO", "NWC"), feature_group_count=CONV_WIDTH)
    return z_b * (y + conv_b.astype(v.dtype))


def s5_group(u_s, A_re, A_im, B_re, B_im, C_re, C_im, D, log_dt, glu_w, glu_b):
    f32 = jnp.float32
    bsz, L, _ = u_s.shape
    u = u_s.astype(f32).reshape(bsz, L, SSM_GROUPS, SSM_GROUP)
    ar, ai = A_re.astype(f32), A_im.astype(f32)
    dt = jnp.exp(log_dt.astype(f32))[:, None]
    mag = jnp.exp(ar * dt)
    ph = ai * dt
    lb_re, lb_im = mag * jnp.cos(ph), mag * jnp.sin(ph)
    nr, ni = lb_re - 1.0, lb_im
    den = ar * ar + ai * ai
    f_re = (nr * ar + ni * ai) / den
    f_im = (ni * ar - nr * ai) / den
    br, bi = B_re.astype(f32), B_im.astype(f32)
    bb_re = f_re[..., None] * br - f_im[..., None] * bi
    bb_im = f_re[..., None] * bi + f_im[..., None] * br
    bu_re = jnp.einsum("blgh,gph->blgp", u, bb_re)
    bu_im = jnp.einsum("blgh,gph->blgp", u, bb_im)
    a_re = jnp.broadcast_to(lb_re[None, None], (1, L, SSM_GROUPS, SSM_STATE))
    a_im = jnp.broadcast_to(lb_im[None, None], (1, L, SSM_GROUPS, SSM_STATE))

    def combine(e1, e2):
        a1r, a1i, b1r, b1i = e1
        a2r, a2i, b2r, b2i = e2
        return (a1r * a2r - a1i * a2i,
                a1r * a2i + a1i * a2r,
                a2r * b1r - a2i * b1i + b2r,
                a2r * b1i + a2i * b1r + b2i)

    _, _, h_re, h_im = lax.associative_scan(combine, (a_re, a_im, bu_re, bu_im), axis=1)
    y = (jnp.einsum("blgp,ghp->blgh", h_re, C_re.astype(f32))
         - jnp.einsum("blgp,ghp->blgh", h_im, C_im.astype(f32))
         + D.astype(f32) * u)
    y = y.reshape(bsz, L, SSM_WIDTH)
    zg = jax.nn.gelu(y)
    out = zg * jax.nn.sigmoid(zg @ glu_w.astype(f32) + glu_b.astype(f32))
    return out.astype(u_s.dtype)


def setup_inputs(seed: int = 0) -> dict:
    key = jax.random.key(seed)
    ks = iter(jax.random.split(key, 40))
    nrm = lambda shape, s: jax.random.normal(next(ks), shape, jnp.float32) * s
    gain = lambda shape: 1.0 + nrm(shape, 0.02)
    n = jnp.arange(SSM_STATE, dtype=jnp.float32)
    log_dt = jax.random.uniform(next(ks), (DEPTH, SSM_GROUPS), jnp.float32,
                                math.log(DT_MIN), math.log(DT_MAX))
    return {
        "x": nrm((BATCH, SEQ, D_MODEL), 1.0),
        "p": nrm((DEPTH, BATCH, SEQ, PLE_DIM), 1.0),
        "ffn1_norm": gain((DEPTH, D_MODEL)),
        "ffn1_w_gate": nrm((DEPTH, D_MODEL, D_FF), D_MODEL ** -0.5),
        "ffn1_w_up": nrm((DEPTH, D_MODEL, D_FF), D_MODEL ** -0.5),
        "ffn1_w_down": nrm((DEPTH, D_FF, D_MODEL), D_FF ** -0.5),
        "mix_norm": gain((DEPTH, D_MODEL)),
        "w_in": nrm((DEPTH, D_MODEL, IN_COLS), D_MODEL ** -0.5),
        "conv_w": nrm((DEPTH, CONV_K, CONV_WIDTH), CONV_K ** -0.5),
        "conv_b": nrm((DEPTH, CONV_WIDTH), 0.02),
        "ssm_A_re": -0.5 + nrm((DEPTH, SSM_GROUPS, SSM_STATE), 0.01),
        "ssm_A_im": math.pi * n + nrm((DEPTH, SSM_GROUPS, SSM_STATE), 0.01),
        "ssm_B_re": nrm((DEPTH, SSM_GROUPS, SSM_STATE, SSM_GROUP), (2 * SSM_GROUP) ** -0.5),
        "ssm_B_im": nrm((DEPTH, SSM_GROUPS, SSM_STATE, SSM_GROUP), (2 * SSM_GROUP) ** -0.5),
        "ssm_C_re": nrm((DEPTH, SSM_GROUPS, SSM_GROUP, SSM_STATE), (2 * SSM_STATE) ** -0.5),
        "ssm_C_im": nrm((DEPTH, SSM_GROUPS, SSM_GROUP, SSM_STATE), (2 * SSM_STATE) ** -0.5),
        "ssm_D": nrm((DEPTH, SSM_GROUPS, SSM_GROUP), 1.0),
        "ssm_log_dt": log_dt,
        "glu_w": nrm((DEPTH, SSM_WIDTH, SSM_WIDTH), SSM_WIDTH ** -0.5),
        "glu_b": nrm((DEPTH, SSM_WIDTH), 0.02),
        "conv_out_norm": gain((DEPTH, CONV_WIDTH)),
        "ssm_out_norm": gain((DEPTH, SSM_WIDTH)),
        "w_out": nrm((DEPTH, MIX_WIDTH, D_MODEL), MIX_WIDTH ** -0.5),
        "ffn2_norm": gain((DEPTH, D_MODEL)),
        "ffn2_w_gate": nrm((DEPTH, D_MODEL, D_FF), D_MODEL ** -0.5),
        "ffn2_w_up": nrm((DEPTH, D_MODEL, D_FF), D_MODEL ** -0.5),
        "ffn2_w_down": nrm((DEPTH, D_FF, D_MODEL), D_FF ** -0.5),
        "ple_norm": gain((DEPTH, D_MODEL)),
        "ple_w_gate": nrm((DEPTH, D_MODEL, D_MODEL), D_MODEL ** -0.5),
        "ple_w_proj": nrm((DEPTH, PLE_DIM, D_MODEL), PLE_DIM ** -0.5),
        "final_norm": gain((D_MODEL,)),
    }


def reference(x, p, ffn1_norm, ffn1_w_gate, ffn1_w_up, ffn1_w_down, mix_norm, w_in,
              conv_w, conv_b, ssm_A_re, ssm_A_im, ssm_B_re, ssm_B_im, ssm_C_re, ssm_C_im,
              ssm_D, ssm_log_dt, glu_w, glu_b, conv_out_norm, ssm_out_norm, w_out,
              ffn2_norm, ffn2_w_gate, ffn2_w_up, ffn2_w_down, ple_norm, ple_w_gate,
              ple_w_proj, final_norm):
    h = x
    for i in range(DEPTH):
        h = h + 0.5 * swiglu(rmsnorm(h, ffn1_norm[i]), ffn1_w_gate[i], ffn1_w_up[i], ffn1_w_down[i])
        z = rmsnorm(h, mix_norm[i]) @ w_in[i]
        z_b = z[..., :CONV_WIDTH]
        z_c = z[..., CONV_WIDTH:2 * CONV_WIDTH]
        z_v = z[..., 2 * CONV_WIDTH:3 * CONV_WIDTH]
        z_s = z[..., 3 * CONV_WIDTH:]
        y_a = short_conv_group(z_b, z_c, z_v, conv_w[i], conv_b[i])
        y_s = s5_group(z_s, ssm_A_re[i], ssm_A_im[i], ssm_B_re[i], ssm_B_im[i],
                       ssm_C_re[i], ssm_C_im[i], ssm_D[i], ssm_log_dt[i], glu_w[i], glu_b[i])
        y = jnp.concatenate([rmsnorm(y_a, conv_out_norm[i]), rmsnorm(y_s, ssm_out_norm[i])], axis=-1)
        h = h + y @ w_out[i]
        h = h + 0.5 * swiglu(rmsnorm(h, ffn2_norm[i]), ffn2_w_gate[i], ffn2_w_up[i], ffn2_w_down[i])
        gate = jax.nn.sigmoid(rmsnorm(h, ple_norm[i]) @ ple_w_gate[i])
        h = h + (p[i] @ ple_w_proj[i]) * gate
    return rmsnorm(h, final_norm)
```

```python
import functools
import math

import jax
import jax.numpy as jnp
from jax import lax
from jax.experimental import pallas as pl
from jax.experimental.pallas import tpu as pltpu

D_MODEL = 1024
BATCH = 8
SEQ = 2048
DEPTH = 4
ROWS = BATCH * SEQ

CONV_WIDTH = 512
CONV_K = 3
SSM_WIDTH = 512
SSM_GROUP = 16
SSM_GROUPS = 32
SSM_STATE = 64
MIX_WIDTH = CONV_WIDTH + SSM_WIDTH
IN_COLS = 3 * CONV_WIDTH + SSM_WIDTH
D_FF = 2816
PLE_DIM = 256
EPS = 1e-6

GROUPS_PER_BLOCK = 16
N_BLOCKS = SSM_GROUPS // GROUPS_PER_BLOCK
BLOCK_IN = GROUPS_PER_BLOCK * SSM_GROUP
BLOCK_STATE = GROUPS_PER_BLOCK * SSM_STATE
STATE_COLS = 2 * SSM_GROUPS * SSM_STATE
SCAN_CHUNK = 512

V7X_VMEM_LIMIT_BYTES = 56 * 1024 * 1024
ROW_TILE = 512
FF_CHUNKS = ((0, 1024), (1024, 1024), (2048, 768))

_BF16 = jnp.bfloat16
_F32 = jnp.float32


def _rms(x, g):
    return x * lax.rsqrt(jnp.mean(x * x, axis=-1, keepdims=True) + EPS) * g


def _dot(a, b):
    return jnp.dot(a, b, preferred_element_type=_F32)


def _resident(shape):
    nd = len(shape)
    return lambda layer: pl.BlockSpec(
        (None,) + tuple(shape), lambda i: (layer,) + (0,) * nd,
        pipeline_mode=pl.Buffered(1))


def _ffn_body(x_ref, g_ref, wg_ref, wu_ref, wd_ref, o_ref):
    x = x_ref[...]
    xn = _rms(x, g_ref[...]).astype(_BF16)
    acc = None
    for start, size in FF_CHUNKS:
        gate = _dot(xn, wg_ref[:, start:start + size])
        up = _dot(xn, wu_ref[:, start:start + size])
        act = (gate * jax.nn.sigmoid(gate) * up).astype(_BF16)
        part = _dot(act, wd_ref[start:start + size, :])
        acc = part if acc is None else acc + part
    o_ref[...] = x + 0.5 * acc


def _ffn(h, layer, norm, w_gate, w_up, w_down):
    row = pl.BlockSpec((ROW_TILE, D_MODEL), lambda i: (i, 0))
    return pl.pallas_call(
        _ffn_body,
        grid=(ROWS // ROW_TILE,),
        in_specs=[row,
                  _resident((1, D_MODEL))(layer),
                  _resident((D_MODEL, D_FF))(layer),
                  _resident((D_MODEL, D_FF))(layer),
                  _resident((D_FF, D_MODEL))(layer)],
        out_specs=row,
        out_shape=jax.ShapeDtypeStruct((ROWS, D_MODEL), _F32),
        compiler_params=pltpu.CompilerParams(
            dimension_semantics=("arbitrary",),
            vmem_limit_bytes=V7X_VMEM_LIMIT_BYTES),
        name="ffn",
    )(h, norm, w_gate, w_up, w_down)


def _mix_body(x_ref, g_ref, win_ref, cw_ref, cb_ref, lr_ref, li_ref, wb_ref, wc_ref,
              d_ref, gw_ref, gb_ref, cn_ref, sn_ref, wo_ref, o_ref,
              state_ref, vbuf_ref, bu_ref, ycat_ref):
    steps = ROW_TILE // BATCH
    halo = (CONV_K - 1) * BATCH

    @pl.when(pl.program_id(0) == 0)
    def _():
        state_ref[...] = jnp.zeros_like(state_ref)
        vbuf_ref[0:halo, :] = jnp.zeros((halo, CONV_WIDTH), _F32)

    x = x_ref[...]
    xn = _rms(x, g_ref[...]).astype(_BF16)
    z = _dot(xn, win_ref[...])
    z_b = z[:, 0:CONV_WIDTH]
    z_c = z[:, CONV_WIDTH:2 * CONV_WIDTH]
    z_v = z[:, 2 * CONV_WIDTH:3 * CONV_WIDTH]
    z_s = z[:, 3 * CONV_WIDTH:]

    vbuf_ref[halo:halo + ROW_TILE, :] = z_c * z_v
    conv = (cw_ref[2:3, :] * vbuf_ref[2 * BATCH:2 * BATCH + ROW_TILE, :]
            + cw_ref[1:2, :] * vbuf_ref[BATCH:BATCH + ROW_TILE, :]
            + cw_ref[0:1, :] * vbuf_ref[0:ROW_TILE, :])
    vbuf_ref[0:halo, :] = vbuf_ref[ROW_TILE:ROW_TILE + halo, :]
    y_a = z_b * (conv + cb_ref[...])
    ycat_ref[:, 0:CONV_WIDTH] = _rms(y_a, cn_ref[...]).astype(_BF16)

    u_bf = z_s.astype(_BF16)
    for j in range(N_BLOCKS):
        bu_ref[:, 2 * BLOCK_STATE * j:2 * BLOCK_STATE * (j + 1)] = _dot(
            u_bf[:, BLOCK_IN * j:BLOCK_IN * (j + 1)], wb_ref[j])

    for j in range(N_BLOCKS):
        for c in range(BLOCK_STATE // SCAN_CHUNK):
            re0 = 2 * BLOCK_STATE * j + SCAN_CHUNK * c
            im0 = re0 + BLOCK_STATE
            l0 = BLOCK_STATE * j + SCAN_CHUNK * c
            lr = lr_ref[:, l0:l0 + SCAN_CHUNK]
            li = li_ref[:, l0:l0 + SCAN_CHUNK]

            def step(s, carry, re0=re0, im0=im0, lr=lr, li=li):
                hr, hi = carry
                r = pl.multiple_of(s * BATCH, BATCH)
                br = bu_ref[pl.ds(r, BATCH), re0:re0 + SCAN_CHUNK]
                bi = bu_ref[pl.ds(r, BATCH), im0:im0 + SCAN_CHUNK]
                nhr = lr * hr - li * hi + br
                nhi = lr * hi + li * hr + bi
                bu_ref[pl.ds(r, BATCH), re0:re0 + SCAN_CHUNK] = nhr
                bu_ref[pl.ds(r, BATCH), im0:im0 + SCAN_CHUNK] = nhi
                return nhr, nhi

            hr, hi = lax.fori_loop(
                0, steps, step,
                (state_ref[:, re0:re0 + SCAN_CHUNK], state_ref[:, im0:im0 + SCAN_CHUNK]),
                unroll=8)
            state_ref[:, re0:re0 + SCAN_CHUNK] = hr
            state_ref[:, im0:im0 + SCAN_CHUNK] = hi

    parts = []
    for j in range(N_BLOCKS):
        hs = bu_ref[:, 2 * BLOCK_STATE * j:2 * BLOCK_STATE * (j + 1)].astype(_BF16)
        parts.append(_dot(hs, wc_ref[j]))
    y = jnp.concatenate(parts, axis=-1) + d_ref[...] * z_s
    c0 = math.sqrt(2.0 / math.pi)
    zg = 0.5 * y * (1.0 + jnp.tanh(c0 * (y + 0.044715 * (y * y * y))))
    gl = _dot(zg.astype(_BF16), gw_ref[...]) + gb_ref[...]
    y_s = zg * jax.nn.sigmoid(gl)
    ycat_ref[:, CONV_WIDTH:] = _rms(y_s, sn_ref[...]).astype(_BF16)

    o_ref[...] = x + _dot(ycat_ref[...], wo_ref[...])


def _mix(h, layer, prm):
    row = pl.BlockSpec((ROW_TILE, D_MODEL), lambda i: (i, 0))
    specs = [row] + [
        _resident(a.shape[1:])(layer) for a in prm]
    return pl.pallas_call(
        _mix_body,
        grid=(ROWS // ROW_TILE,),
        in_specs=specs,
        out_specs=row,
        out_shape=jax.ShapeDtypeStruct((ROWS, D_MODEL), _F32),
        scratch_shapes=[
            pltpu.VMEM((BATCH, STATE_COLS), _F32),
            pltpu.VMEM((ROW_TILE + (CONV_K - 1) * BATCH, CONV_WIDTH), _F32),
            pltpu.VMEM((ROW_TILE, STATE_COLS), _F32),
            pltpu.VMEM((ROW_TILE, MIX_WIDTH), _BF16),
        ],
        compiler_params=pltpu.CompilerParams(
            dimension_semantics=("arbitrary",),
            vmem_limit_bytes=V7X_VMEM_LIMIT_BYTES),
        name="mix",
    )(h, *prm)


def _ple_body(x_ref, p_ref, g_ref, wg_ref, wp_ref, fg_ref, o_ref, *, final):
    x = x_ref[...]
    xn = _rms(x, g_ref[...]).astype(_BF16)
    gate = jax.nn.sigmoid(_dot(xn, wg_ref[...]))
    out = x + _dot(p_ref[...], wp_ref[...]) * gate
    if final:
        out = _rms(out, fg_ref[...])
    o_ref[...] = out


def _ple(h, p_rows, layer, norm, w_gate, w_proj, final_norm, final):
    row = pl.BlockSpec((ROW_TILE, D_MODEL), lambda i: (i, 0))
    return pl.pallas_call(
        functools.partial(_ple_body, final=final),
        grid=(ROWS // ROW_TILE,),
        in_specs=[row,
                  pl.BlockSpec((None, ROW_TILE, PLE_DIM), lambda i: (layer, i, 0)),
                  _resident((1, D_MODEL))(layer),
                  _resident((D_MODEL, D_MODEL))(layer),
                  _resident((PLE_DIM, D_MODEL))(layer),
                  pl.BlockSpec((1, D_MODEL), lambda i: (0, 0))],
        out_specs=row,
        out_shape=jax.ShapeDtypeStruct((ROWS, D_MODEL), _F32),
        compiler_params=pltpu.CompilerParams(
            dimension_semantics=("arbitrary",),
            vmem_limit_bytes=V7X_VMEM_LIMIT_BYTES),
        name="ple",
    )(h, p_rows, norm, w_gate, w_proj, final_norm)


def _block_diag(w):
    d, _, a, b = w.shape
    w = w.reshape(d, N_BLOCKS, GROUPS_PER_BLOCK, a, b)
    eye = jnp.eye(GROUPS_PER_BLOCK, dtype=w.dtype)
    out = jnp.einsum("djgab,gk->djgakb", w, eye)
    return out.reshape(d, N_BLOCKS, GROUPS_PER_BLOCK * a, GROUPS_PER_BLOCK * b)


def _s5_params(A_re, A_im, B_re, B_im, C_re, C_im, log_dt):
    dt = jnp.exp(log_dt)[..., None]
    mag = jnp.exp(A_re * dt)
    ph = A_im * dt
    lb_re, lb_im = mag * jnp.cos(ph), mag * jnp.sin(ph)
    nr, ni = lb_re - 1.0, lb_im
    den = A_re * A_re + A_im * A_im
    f_re = (nr * A_re + ni * A_im) / den
    f_im = (ni * A_re - nr * A_im) / den
    bb_re = f_re[..., None] * B_re - f_im[..., None] * B_im
    bb_im = f_re[..., None] * B_im + f_im[..., None] * B_re
    wb = jnp.concatenate([_block_diag(jnp.swapaxes(bb_re, -1, -2)),
                          _block_diag(jnp.swapaxes(bb_im, -1, -2))], axis=-1)
    wc = jnp.concatenate([_block_diag(jnp.swapaxes(C_re, -1, -2)),
                          _block_diag(-jnp.swapaxes(C_im, -1, -2))], axis=-2)
    lam_shape = (DEPTH, BATCH, SSM_GROUPS * SSM_STATE)
    lr = jnp.broadcast_to(lb_re.reshape(DEPTH, 1, -1), lam_shape)
    li = jnp.broadcast_to(lb_im.reshape(DEPTH, 1, -1), lam_shape)
    return lr, li, wb.astype(_BF16), wc.astype(_BF16)


def kernel(x, p, ffn1_norm, ffn1_w_gate, ffn1_w_up, ffn1_w_down, mix_norm, w_in, conv_w, conv_b, ssm_A_re, ssm_A_im, ssm_B_re, ssm_B_im, ssm_C_re, ssm_C_im, ssm_D, ssm_log_dt, glu_w, glu_b, conv_out_norm, ssm_out_norm, w_out, ffn2_norm, ffn2_w_gate, ffn2_w_up, ffn2_w_down, ple_norm, ple_w_gate, ple_w_proj, final_norm):
    bf = lambda a: a.astype(_BF16)
    vec = lambda a: a.reshape(DEPTH, 1, -1)

    h = jnp.transpose(x, (1, 0, 2)).reshape(ROWS, D_MODEL)
    p_rows = bf(jnp.transpose(p, (0, 2, 1, 3))).reshape(DEPTH, ROWS, PLE_DIM)

    lr, li, wb, wc = _s5_params(ssm_A_re, ssm_A_im, ssm_B_re, ssm_B_im,
                                ssm_C_re, ssm_C_im, ssm_log_dt)
    mix_prm = (vec(mix_norm), bf(w_in), conv_w, vec(conv_b), lr, li, wb, wc,
               vec(ssm_D), bf(glu_w), vec(glu_b), vec(conv_out_norm),
               vec(ssm_out_norm), bf(w_out))
    f1 = (vec(ffn1_norm), bf(ffn1_w_gate), bf(ffn1_w_up), bf(ffn1_w_down))
    f2 = (vec(ffn2_norm), bf(ffn2_w_gate), bf(ffn2_w_up), bf(ffn2_w_down))
    ple_prm = (vec(ple_norm), bf(ple_w_gate), bf(ple_w_proj))
    fin = final_norm.reshape(1, D_MODEL)

    for i in range(DEPTH):
        h = _ffn(h, i, *f1)
        h = _mix(h, i, mix_prm)
        h = _ffn(h, i, *f2)
        h = _ple(h, p_rows, i, *ple_prm, fin, i == DEPTH - 1)

    return jnp.transpose(h.reshape(SEQ, BATCH, D_MODEL), (1, 0, 2))
```

```python
import functools
import math

import jax
import jax.numpy as jnp
from jax import lax
from jax.experimental import pallas as pl
from jax.experimental.pallas import tpu as pltpu

D_MODEL = 1024
BATCH = 8
SEQ = 2048
DEPTH = 4
ROWS = BATCH * SEQ

CONV_WIDTH = 512
CONV_K = 3
SSM_WIDTH = 512
SSM_GROUP = 16
SSM_GROUPS = 32
SSM_STATE = 64
MIX_WIDTH = CONV_WIDTH + SSM_WIDTH
IN_COLS = 3 * CONV_WIDTH + SSM_WIDTH
D_FF = 2816
PLE_DIM = 256
EPS = 1e-6

GROUPS_PER_BLOCK = 16
N_BLOCKS = SSM_GROUPS // GROUPS_PER_BLOCK
BLOCK_IN = GROUPS_PER_BLOCK * SSM_GROUP
BLOCK_STATE = GROUPS_PER_BLOCK * SSM_STATE
STATE_COLS = 2 * SSM_GROUPS * SSM_STATE
SCAN_CHUNK = 1024

V7X_LANES = 128
V7X_VMEM_LIMIT_BYTES = 56 * 1024 * 1024
ROW_TILE = 512
FF_CHUNKS = ((0, 1024), (1024, 1024), (2048, 768))

_BF16 = jnp.bfloat16
_F32 = jnp.float32


def _rms(x, g):
    return x * lax.rsqrt(jnp.mean(x * x, axis=-1, keepdims=True) + EPS) * g


def _dot(a, b):
    return jnp.dot(a, b, preferred_element_type=_F32)


def _resident(shape):
    nd = len(shape)
    return lambda layer: pl.BlockSpec(
        (None,) + tuple(shape), lambda i: (layer,) + (0,) * nd,
        pipeline_mode=pl.Buffered(1))


def _row_spec(width):
    return pl.BlockSpec((ROW_TILE, width), lambda i: (i, 0))


def _params():
    return pltpu.CompilerParams(dimension_semantics=("arbitrary",),
                                vmem_limit_bytes=V7X_VMEM_LIMIT_BYTES)


def _slab_scratch(width):
    return pltpu.VMEM((width // V7X_LANES, ROW_TILE, V7X_LANES), _F32)


def _rows_from_batch_major(src_ref, slab_ref):
    steps = ROW_TILE // BATCH
    n_slabs = slab_ref.shape[0]
    for b in range(BATCH):
        for c in range(n_slabs):
            slab_ref[c, pl.ds(b, steps, stride=BATCH), :] = (
                src_ref[b, :, c * V7X_LANES:(c + 1) * V7X_LANES])
    return jnp.concatenate([slab_ref[c] for c in range(n_slabs)], axis=-1)


def _rows_to_batch_major(val, slab_ref, dst_ref):
    steps = ROW_TILE // BATCH
    n_slabs = slab_ref.shape[0]
    for c in range(n_slabs):
        slab_ref[c] = val[:, c * V7X_LANES:(c + 1) * V7X_LANES]
    for b in range(BATCH):
        for c in range(n_slabs):
            dst_ref[b, :, c * V7X_LANES:(c + 1) * V7X_LANES] = (
                slab_ref[c, pl.ds(b, steps, stride=BATCH), :])


def _ffn_body(*refs, x_batch_major, with_ple, final):
    refs = list(refs)
    x_ref, g_ref, wg_ref, wu_ref, wd_ref = refs[:5]
    rest = refs[5:]
    if with_ple:
        p_ref, pn_ref, pg_ref, pp_ref, fn_ref = rest[:5]
        rest = rest[5:]
    o_ref = rest[0]
    slabs = iter(rest[1:])

    if x_batch_major:
        x = _rows_from_batch_major(x_ref, next(slabs))
    else:
        x = x_ref[...]
    xn = _rms(x, g_ref[...]).astype(_BF16)
    acc = None
    for start, size in FF_CHUNKS:
        gate = _dot(xn, wg_ref[:, start:start + size])
        up = _dot(xn, wu_ref[:, start:start + size])
        act = (gate * jax.nn.sigmoid(gate) * up).astype(_BF16)
        part = _dot(act, wd_ref[start:start + size, :])
        acc = part if acc is None else acc + part
    out = x + 0.5 * acc

    if with_ple:
        p_rows = _rows_from_batch_major(p_ref, next(slabs)).astype(_BF16)
        gate = jax.nn.sigmoid(_dot(_rms(out, pn_ref[...]).astype(_BF16), pg_ref[...]))
        out = out + _dot(p_rows, pp_ref[...]) * gate
    if final:
        _rows_to_batch_major(_rms(out, fn_ref[...]), next(slabs), o_ref)
    else:
        o_ref[...] = out


def _ffn(h, layer, ffn_prm, *, x_batch_major=False, ple=None, final=False):
    batch_major = lambda width: pl.BlockSpec(
        (BATCH, ROW_TILE // BATCH, width), lambda i: (0, i, 0))
    in_specs = [batch_major(D_MODEL) if x_batch_major else _row_spec(D_MODEL),
                _resident((1, D_MODEL))(layer),
                _resident((D_MODEL, D_FF))(layer),
                _resident((D_MODEL, D_FF))(layer),
                _resident((D_FF, D_MODEL))(layer)]
    args = [h, *ffn_prm]
    scratch = []
    if x_batch_major:
        scratch.append(_slab_scratch(D_MODEL))
    if ple is not None:
        p, ple_norm, ple_w_gate, ple_w_proj, final_norm = ple
        in_specs += [pl.BlockSpec((None, BATCH, ROW_TILE // BATCH, PLE_DIM),
                                  lambda i: (layer, 0, i, 0)),
                     _resident((1, D_MODEL))(layer),
                     _resident((D_MODEL, D_MODEL))(layer),
                     _resident((PLE_DIM, D_MODEL))(layer),
                     pl.BlockSpec((1, D_MODEL), lambda i: (0, 0))]
        args += [p, ple_norm, ple_w_gate, ple_w_proj, final_norm]
        scratch.append(_slab_scratch(PLE_DIM))
    if final:
        scratch.append(_slab_scratch(D_MODEL))
        out_specs = batch_major(D_MODEL)
        out_shape = jax.ShapeDtypeStruct((BATCH, SEQ, D_MODEL), _F32)
    else:
        out_specs = _row_spec(D_MODEL)
        out_shape = jax.ShapeDtypeStruct((ROWS, D_MODEL), _F32)
    return pl.pallas_call(
        functools.partial(_ffn_body, x_batch_major=x_batch_major,
                          with_ple=ple is not None, final=final),
        grid=(ROWS // ROW_TILE,),
        in_specs=in_specs,
        out_specs=out_specs,
        out_shape=out_shape,
        scratch_shapes=scratch,
        compiler_params=_params(),
        name="ffn_ple" if ple is not None else "ffn",
    )(*args)


def _mix_body(x_ref, g_ref, win_ref, cw_ref, cb_ref, lr_ref, li_ref, wb_ref, wc_ref,
              d_ref, gw_ref, gb_ref, cn_ref, sn_ref, wo_ref, o_ref,
              state_ref, vbuf_ref, bu_ref, hs_ref, ycat_ref):
    steps = ROW_TILE // BATCH
    halo = (CONV_K - 1) * BATCH

    @pl.when(pl.program_id(0) == 0)
    def _():
        state_ref[...] = jnp.zeros_like(state_ref)
        vbuf_ref[0:halo, :] = jnp.zeros((halo, CONV_WIDTH), _F32)

    x = x_ref[...]
    xn = _rms(x, g_ref[...]).astype(_BF16)
    z = _dot(xn, win_ref[...])
    z_b = z[:, 0:CONV_WIDTH]
    z_c = z[:, CONV_WIDTH:2 * CONV_WIDTH]
    z_v = z[:, 2 * CONV_WIDTH:3 * CONV_WIDTH]
    z_s = z[:, 3 * CONV_WIDTH:]

    vbuf_ref[halo:halo + ROW_TILE, :] = z_c * z_v
    conv = (cw_ref[2:3, :] * vbuf_ref[2 * BATCH:2 * BATCH + ROW_TILE, :]
            + cw_ref[1:2, :] * vbuf_ref[BATCH:BATCH + ROW_TILE, :]
            + cw_ref[0:1, :] * vbuf_ref[0:ROW_TILE, :])
    vbuf_ref[0:halo, :] = vbuf_ref[ROW_TILE:ROW_TILE + halo, :]
    y_a = z_b * (conv + cb_ref[...])
    ycat_ref[:, 0:CONV_WIDTH] = _rms(y_a, cn_ref[...]).astype(_BF16)

    u_bf = z_s.astype(_BF16)
    for j in range(N_BLOCKS):
        bu_ref[:, 2 * BLOCK_STATE * j:2 * BLOCK_STATE * (j + 1)] = _dot(
            u_bf[:, BLOCK_IN * j:BLOCK_IN * (j + 1)], wb_ref[j])

    for j in range(N_BLOCKS):
        for c in range(BLOCK_STATE // SCAN_CHUNK):
            re0 = 2 * BLOCK_STATE * j + SCAN_CHUNK * c
            im0 = re0 + BLOCK_STATE
            l0 = BLOCK_STATE * j + SCAN_CHUNK * c

            def step(s, carry, re0=re0, im0=im0, l0=l0):
                hr, hi = carry
                r = pl.multiple_of(s * BATCH, BATCH)
                lr = lr_ref[:, l0:l0 + SCAN_CHUNK]
                li = li_ref[:, l0:l0 + SCAN_CHUNK]
                nhr = lr * hr - li * hi + bu_ref[pl.ds(r, BATCH), re0:re0 + SCAN_CHUNK]
                nhi = lr * hi + li * hr + bu_ref[pl.ds(r, BATCH), im0:im0 + SCAN_CHUNK]
                hs_ref[pl.ds(r, BATCH), re0:re0 + SCAN_CHUNK] = nhr
                hs_ref[pl.ds(r, BATCH), im0:im0 + SCAN_CHUNK] = nhi
                return nhr, nhi

            hr, hi = lax.fori_loop(
                0, steps, step,
                (state_ref[:, re0:re0 + SCAN_CHUNK], state_ref[:, im0:im0 + SCAN_CHUNK]),
                unroll=4)
            state_ref[:, re0:re0 + SCAN_CHUNK] = hr
            state_ref[:, im0:im0 + SCAN_CHUNK] = hi

    parts = []
    for j in range(N_BLOCKS):
        hs = hs_ref[:, 2 * BLOCK_STATE * j:2 * BLOCK_STATE * (j + 1)].astype(_BF16)
        parts.append(_dot(hs, wc_ref[j]))
    y = jnp.concatenate(parts, axis=-1) + d_ref[...] * z_s
    c0 = math.sqrt(2.0 / math.pi)
    zg = 0.5 * y * (1.0 + jnp.tanh(c0 * (y + 0.044715 * (y * y * y))))
    gl = _dot(zg.astype(_BF16), gw_ref[...]) + gb_ref[...]
    y_s = zg * jax.nn.sigmoid(gl)
    ycat_ref[:, CONV_WIDTH:] = _rms(y_s, sn_ref[...]).astype(_BF16)

    o_ref[...] = x + _dot(ycat_ref[...], wo_ref[...])


def _mix(h, layer, prm):
    specs = [_row_spec(D_MODEL)] + [_resident(a.shape[1:])(layer) for a in prm]
    return pl.pallas_call(
        _mix_body,
        grid=(ROWS // ROW_TILE,),
        in_specs=specs,
        out_specs=_row_spec(D_MODEL),
        out_shape=jax.ShapeDtypeStruct((ROWS, D_MODEL), _F32),
        scratch_shapes=[
            pltpu.VMEM((BATCH, STATE_COLS), _F32),
            pltpu.VMEM((ROW_TILE + (CONV_K - 1) * BATCH, CONV_WIDTH), _F32),
            pltpu.VMEM((ROW_TILE, STATE_COLS), _F32),
            pltpu.VMEM((ROW_TILE, STATE_COLS), _F32),
            pltpu.VMEM((ROW_TILE, MIX_WIDTH), _BF16),
        ],
        compiler_params=_params(),
        name="mix",
    )(h, *prm)


def _block_diag(w):
    d, _, a, b = w.shape
    w = w.reshape(d, N_BLOCKS, GROUPS_PER_BLOCK, a, b)
    eye = jnp.eye(GROUPS_PER_BLOCK, dtype=w.dtype)
    out = jnp.einsum("djgab,gk->djgakb", w, eye)
    return out.reshape(d, N_BLOCKS, GROUPS_PER_BLOCK * a, GROUPS_PER_BLOCK * b)


def _s5_params(A_re, A_im, B_re, B_im, C_re, C_im, log_dt):
    dt = jnp.exp(log_dt)[..., None]
    mag = jnp.exp(A_re * dt)
    ph = A_im * dt
    lb_re, lb_im = mag * jnp.cos(ph), mag * jnp.sin(ph)
    nr, ni = lb_re - 1.0, lb_im
    den = A_re * A_re + A_im * A_im
    f_re = (nr * A_re + ni * A_im) / den
    f_im = (ni * A_re - nr * A_im) / den
    bb_re = f_re[..., None] * B_re - f_im[..., None] * B_im
    bb_im = f_re[..., None] * B_im + f_im[..., None] * B_re
    wb = jnp.concatenate([_block_diag(jnp.swapaxes(bb_re, -1, -2)),
                          _block_diag(jnp.swapaxes(bb_im, -1, -2))], axis=-1)
    wc = jnp.concatenate([_block_diag(jnp.swapaxes(C_re, -1, -2)),
                          _block_diag(-jnp.swapaxes(C_im, -1, -2))], axis=-2)
    lam_shape = (DEPTH, BATCH, SSM_GROUPS * SSM_STATE)
    lr = jnp.broadcast_to(lb_re.reshape(DEPTH, 1, -1), lam_shape)
    li = jnp.broadcast_to(lb_im.reshape(DEPTH, 1, -1), lam_shape)
    return lr, li, wb.astype(_BF16), wc.astype(_BF16)


def kernel(x, p, ffn1_norm, ffn1_w_gate, ffn1_w_up, ffn1_w_down, mix_norm, w_in, conv_w, conv_b, ssm_A_re, ssm_A_im, ssm_B_re, ssm_B_im, ssm_C_re, ssm_C_im, ssm_D, ssm_log_dt, glu_w, glu_b, conv_out_norm, ssm_out_norm, w_out, ffn2_norm, ffn2_w_gate, ffn2_w_up, ffn2_w_down, ple_norm, ple_w_gate, ple_w_proj, final_norm):
    bf = lambda a: a.astype(_BF16)
    vec = lambda a: a.reshape(DEPTH, 1, -1)

    lr, li, wb, wc = _s5_params(ssm_A_re, ssm_A_im, ssm_B_re, ssm_B_im,
                                ssm_C_re, ssm_C_im, ssm_log_dt)
    mix_prm = (vec(mix_norm), bf(w_in), conv_w, vec(conv_b), lr, li, wb, wc,
               vec(ssm_D), bf(glu_w), vec(glu_b), vec(conv_out_norm),
               vec(ssm_out_norm), bf(w_out))
    f1 = (vec(ffn1_norm), bf(ffn1_w_gate), bf(ffn1_w_up), bf(ffn1_w_down))
    f2 = (vec(ffn2_norm), bf(ffn2_w_gate), bf(ffn2_w_up), bf(ffn2_w_down))
    ple = (p, vec(ple_norm), bf(ple_w_gate), bf(ple_w_proj), final_norm.reshape(1, D_MODEL))

    h = x
    for i in range(DEPTH):
        h = _ffn(h, i, f1, x_batch_major=(i == 0))
        h = _mix(h, i, mix_prm)
        h = _ffn(h, i, f2, ple=ple, final=(i == DEPTH - 1))
    return h
```

```python
import functools
import math

import jax
import jax.numpy as jnp
from jax import lax
from jax.experimental import pallas as pl
from jax.experimental.pallas import tpu as pltpu

D_MODEL = 1024
BATCH = 8
SEQ = 2048
DEPTH = 4
ROWS = BATCH * SEQ

CONV_WIDTH = 512
CONV_K = 3
SSM_WIDTH = 512
SSM_GROUP = 16
SSM_GROUPS = 32
SSM_STATE = 64
MIX_WIDTH = CONV_WIDTH + SSM_WIDTH
IN_COLS = 3 * CONV_WIDTH + SSM_WIDTH
D_FF = 2816
PLE_DIM = 256
EPS = 1e-6

GROUPS_PER_BLOCK = 16
N_BLOCKS = SSM_GROUPS // GROUPS_PER_BLOCK
BLOCK_IN = GROUPS_PER_BLOCK * SSM_GROUP
BLOCK_STATE = GROUPS_PER_BLOCK * SSM_STATE
STATE_COLS = 2 * SSM_GROUPS * SSM_STATE
SCAN_CHUNK = 1024

V7X_LANES = 128
V7X_VMEM_LIMIT_BYTES = 56 * 1024 * 1024
ROW_TILE = 512
FF_CHUNKS = ((0, 1024), (1024, 1024), (2048, 768))
CAST_BLOCKS = 16

_BF16 = jnp.bfloat16
_F32 = jnp.float32


def _rms(x, g):
    return x * lax.rsqrt(jnp.mean(x * x, axis=-1, keepdims=True) + EPS) * g


def _dot(a, b):
    return jnp.dot(a, b, preferred_element_type=_F32)


def _stacked(shape, layer):
    nd = len(shape)
    return pl.BlockSpec((None,) + tuple(shape), lambda i: (layer,) + (0,) * nd,
                        pipeline_mode=pl.Buffered(1))


def _whole(shape):
    nd = len(shape)
    return pl.BlockSpec(tuple(shape), lambda i: (0,) * nd, pipeline_mode=pl.Buffered(1))


def _row_spec(width):
    return pl.BlockSpec((ROW_TILE, width), lambda i: (i, 0))


def _params():
    return pltpu.CompilerParams(dimension_semantics=("arbitrary",),
                                vmem_limit_bytes=V7X_VMEM_LIMIT_BYTES)


def _slab_scratch(width):
    return pltpu.VMEM((width // V7X_LANES, ROW_TILE, V7X_LANES), _F32)


def _rows_from_batch_major(src_ref, slab_ref):
    steps = ROW_TILE // BATCH
    n_slabs = slab_ref.shape[0]
    for b in range(BATCH):
        for c in range(n_slabs):
            slab_ref[c, pl.ds(b, steps, stride=BATCH), :] = (
                src_ref[b, :, c * V7X_LANES:(c + 1) * V7X_LANES])
    return jnp.concatenate([slab_ref[c] for c in range(n_slabs)], axis=-1)


def _rows_to_batch_major(val, slab_ref, dst_ref):
    steps = ROW_TILE // BATCH
    n_slabs = slab_ref.shape[0]
    for c in range(n_slabs):
        slab_ref[c] = val[:, c * V7X_LANES:(c + 1) * V7X_LANES]
    for b in range(BATCH):
        for c in range(n_slabs):
            dst_ref[b, :, c * V7X_LANES:(c + 1) * V7X_LANES] = (
                slab_ref[c, pl.ds(b, steps, stride=BATCH), :])


def _cast_plan(weights, layer):
    grid_steps = ROWS // ROW_TILE
    blocks = min(CAST_BLOCKS, grid_steps)
    per = grid_steps // blocks
    in_specs, out_specs, out_shapes = [], [], []
    for w in weights:
        _, r, c = w.shape
        in_specs.append(pl.BlockSpec((None, r // blocks, c), lambda i: (layer, i // per, 0)))
        out_specs.append(pl.BlockSpec((r // blocks, c), lambda i: (i // per, 0)))
        out_shapes.append(jax.ShapeDtypeStruct((r, c), _BF16))
    return in_specs, out_specs, out_shapes


def _split_refs(refs, n_in, n_cast):
    refs = list(refs)
    main = refs[:n_in]
    cast_in = refs[n_in:n_in + n_cast]
    o_ref = refs[n_in + n_cast]
    cast_out = refs[n_in + n_cast + 1:n_in + 2 * n_cast + 1]
    scratch = refs[n_in + 2 * n_cast + 1:]

    def convert():
        for src, dst in zip(cast_in, cast_out):
            dst[...] = src[...].astype(_BF16)

    return main, o_ref, scratch, convert


def _ffn_body(*refs, n_in, n_cast, x_batch_major, with_ple, final):
    main, o_ref, scratch, convert_next_weights = _split_refs(refs, n_in, n_cast)
    x_ref, g_ref, wg_ref, wu_ref, wd_ref = main[:5]
    slabs = iter(scratch)

    if x_batch_major:
        x = _rows_from_batch_major(x_ref, next(slabs))
    else:
        x = x_ref[...]
    xn = _rms(x, g_ref[...]).astype(_BF16)
    acc = None
    for start, size in FF_CHUNKS:
        gate = _dot(xn, wg_ref[:, start:start + size])
        up = _dot(xn, wu_ref[:, start:start + size])
        act = (gate * jax.nn.sigmoid(gate) * up).astype(_BF16)
        part = _dot(act, wd_ref[start:start + size, :])
        if acc is None:
            acc = part
            convert_next_weights()
        else:
            acc = acc + part
    out = x + 0.5 * acc

    if with_ple:
        p_ref, pn_ref, pg_ref, pp_ref, fn_ref = main[5:]
        p_rows = _rows_from_batch_major(p_ref, next(slabs)).astype(_BF16)
        gate = jax.nn.sigmoid(_dot(_rms(out, pn_ref[...]).astype(_BF16), pg_ref[...]))
        out = out + _dot(p_rows, pp_ref[...]) * gate
    if final:
        _rows_to_batch_major(_rms(out, fn_ref[...]), next(slabs), o_ref)
    else:
        o_ref[...] = out


def _ffn(h, layer, norm, w_bf, *, x_batch_major=False, ple=None, final=False, cast=()):
    batch_major = lambda width: pl.BlockSpec(
        (BATCH, ROW_TILE // BATCH, width), lambda i: (0, i, 0))
    in_specs = [batch_major(D_MODEL) if x_batch_major else _row_spec(D_MODEL),
                _stacked((1, D_MODEL), layer),
                _whole((D_MODEL, D_FF)), _whole((D_MODEL, D_FF)), _whole((D_FF, D_MODEL))]
    args = [h, norm, *w_bf]
    scratch = []
    if x_batch_major:
        scratch.append(_slab_scratch(D_MODEL))
    if ple is not None:
        p, ple_norm, ple_w_gate, ple_w_proj, final_norm = ple
        in_specs += [pl.BlockSpec((None, BATCH, ROW_TILE // BATCH, PLE_DIM),
                                  lambda i: (layer, 0, i, 0)),
                     _stacked((1, D_MODEL), layer),
                     _whole((D_MODEL, D_MODEL)), _whole((PLE_DIM, D_MODEL)),
                     pl.BlockSpec((1, D_MODEL), lambda i: (0, 0))]
        args += [p, ple_norm, ple_w_gate, ple_w_proj, final_norm]
        scratch.append(_slab_scratch(PLE_DIM))
    if final:
        scratch.append(_slab_scratch(D_MODEL))
        out_spec = batch_major(D_MODEL)
        out_shape = jax.ShapeDtypeStruct((BATCH, SEQ, D_MODEL), _F32)
    else:
        out_spec = _row_spec(D_MODEL)
        out_shape = jax.ShapeDtypeStruct((ROWS, D_MODEL), _F32)
    cast_w, cast_layer = cast if cast else ((), 0)
    c_in, c_out, c_shapes = _cast_plan(cast_w, cast_layer)
    outs = pl.pallas_call(
        functools.partial(_ffn_body, n_in=len(args), n_cast=len(cast_w),
                          x_batch_major=x_batch_major, with_ple=ple is not None, final=final),
        grid=(ROWS // ROW_TILE,),
        in_specs=in_specs + c_in,
        out_specs=[out_spec] + c_out,
        out_shape=[out_shape] + c_shapes,
        scratch_shapes=scratch,
        compiler_params=_params(),
        name="ffn_ple" if ple is not None else "ffn",
    )(*args, *cast_w)
    return outs[0], tuple(outs[1:])


def _mix_body(*refs, n_in, n_cast):
    main, o_ref, scratch, convert_next_weights = _split_refs(refs, n_in, n_cast)
    (x_ref, g_ref, win_ref, cw_ref, cb_ref, lr_ref, li_ref, wb_ref, wc_ref,
     d_ref, gw_ref, gb_ref, cn_ref, sn_ref, wo_ref) = main
    state_ref, vbuf_ref, bu_ref, hs_ref, ycat_ref = scratch
    steps = ROW_TILE // BATCH
    halo = (CONV_K - 1) * BATCH

    @pl.when(pl.program_id(0) == 0)
    def _():
        state_ref[...] = jnp.zeros_like(state_ref)
        vbuf_ref[0:halo, :] = jnp.zeros((halo, CONV_WIDTH), _F32)

    x = x_ref[...]
    xn = _rms(x, g_ref[...]).astype(_BF16)
    z = _dot(xn, win_ref[...])
    convert_next_weights()
    z_b = z[:, 0:CONV_WIDTH]
    z_c = z[:, CONV_WIDTH:2 * CONV_WIDTH]
    z_v = z[:, 2 * CONV_WIDTH:3 * CONV_WIDTH]
    z_s = z[:, 3 * CONV_WIDTH:]

    vbuf_ref[halo:halo + ROW_TILE, :] = z_c * z_v
    conv = (cw_ref[2:3, :] * vbuf_ref[2 * BATCH:2 * BATCH + ROW_TILE, :]
            + cw_ref[1:2, :] * vbuf_ref[BATCH:BATCH + ROW_TILE, :]
            + cw_ref[0:1, :] * vbuf_ref[0:ROW_TILE, :])
    vbuf_ref[0:halo, :] = vbuf_ref[ROW_TILE:ROW_TILE + halo, :]
    y_a = z_b * (conv + cb_ref[...])
    ycat_ref[:, 0:CONV_WIDTH] = _rms(y_a, cn_ref[...]).astype(_BF16)

    u_bf = z_s.astype(_BF16)
    for j in range(N_BLOCKS):
        bu_ref[:, 2 * BLOCK_STATE * j:2 * BLOCK_STATE * (j + 1)] = _dot(
            u_bf[:, BLOCK_IN * j:BLOCK_IN * (j + 1)], wb_ref[j])

    for j in range(N_BLOCKS):
        for c in range(BLOCK_STATE // SCAN_CHUNK):
            re0 = 2 * BLOCK_STATE * j + SCAN_CHUNK * c
            im0 = re0 + BLOCK_STATE
            l0 = BLOCK_STATE * j + SCAN_CHUNK * c

            def step(s, carry, re0=re0, im0=im0, l0=l0):
                hr, hi = carry
                r = pl.multiple_of(s * BATCH, BATCH)
                lr = lr_ref[:, l0:l0 + SCAN_CHUNK]
                li = li_ref[:, l0:l0 + SCAN_CHUNK]
                nhr = lr * hr - li * hi + bu_ref[pl.ds(r, BATCH), re0:re0 + SCAN_CHUNK]
                nhi = lr * hi + li * hr + bu_ref[pl.ds(r, BATCH), im0:im0 + SCAN_CHUNK]
                hs_ref[pl.ds(r, BATCH), re0:re0 + SCAN_CHUNK] = nhr
                hs_ref[pl.ds(r, BATCH), im0:im0 + SCAN_CHUNK] = nhi
                return nhr, nhi

            hr, hi = lax.fori_loop(
                0, steps, step,
                (state_ref[:, re0:re0 + SCAN_CHUNK], state_ref[:, im0:im0 + SCAN_CHUNK]),
                unroll=4)
            state_ref[:, re0:re0 + SCAN_CHUNK] = hr
            state_ref[:, im0:im0 + SCAN_CHUNK] = hi

    parts = []
    for j in range(N_BLOCKS):
        hs = hs_ref[:, 2 * BLOCK_STATE * j:2 * BLOCK_STATE * (j + 1)].astype(_BF16)
        parts.append(_dot(hs, wc_ref[j]))
    y = jnp.concatenate(parts, axis=-1) + d_ref[...] * z_s
    c0 = math.sqrt(2.0 / math.pi)
    zg = 0.5 * y * (1.0 + jnp.tanh(c0 * (y + 0.044715 * (y * y * y))))
    gl = _dot(zg.astype(_BF16), gw_ref[...]) + gb_ref[...]
    y_s = zg * jax.nn.sigmoid(gl)
    ycat_ref[:, CONV_WIDTH:] = _rms(y_s, sn_ref[...]).astype(_BF16)

    o_ref[...] = x + _dot(ycat_ref[...], wo_ref[...])


def _mix(h, layer, stacked, w_bf, cast):
    (mix_norm, conv_w, conv_b, lr, li, wb, wc, ssm_d, glu_b, conv_norm, ssm_norm) = stacked
    w_in, glu_w, w_out = w_bf
    st = lambda a: _stacked(a.shape[1:], layer)
    wh = lambda a: _whole(a.shape)
    args = [h, mix_norm, w_in, conv_w, conv_b, lr, li, wb, wc, ssm_d, glu_w, glu_b,
            conv_norm, ssm_norm, w_out]
    in_specs = [_row_spec(D_MODEL), st(mix_norm), wh(w_in), st(conv_w), st(conv_b), st(lr),
                st(li), st(wb), st(wc), st(ssm_d), wh(glu_w), st(glu_b), st(conv_norm),
                st(ssm_norm), wh(w_out)]
    cast_w, cast_layer = cast
    c_in, c_out, c_shapes = _cast_plan(cast_w, cast_layer)
    outs = pl.pallas_call(
        functools.partial(_mix_body, n_in=len(args), n_cast=len(cast_w)),
        grid=(ROWS // ROW_TILE,),
        in_specs=in_specs + c_in,
        out_specs=[_row_spec(D_MODEL)] + c_out,
        out_shape=[jax.ShapeDtypeStruct((ROWS, D_MODEL), _F32)] + c_shapes,
        scratch_shapes=[
            pltpu.VMEM((BATCH, STATE_COLS), _F32),
            pltpu.VMEM((ROW_TILE + (CONV_K - 1) * BATCH, CONV_WIDTH), _F32),
            pltpu.VMEM((ROW_TILE, STATE_COLS), _F32),
            pltpu.VMEM((ROW_TILE, STATE_COLS), _F32),
            pltpu.VMEM((ROW_TILE, MIX_WIDTH), _BF16),
        ],
        compiler_params=_params(),
        name="mix",
    )(*args, *cast_w)
    return outs[0], tuple(outs[1:])


def _block_diag(w):
    d, _, a, b = w.shape
    rows, cols = GROUPS_PER_BLOCK * a, GROUPS_PER_BLOCK * b
    tiled = jnp.tile(w.reshape(d, N_BLOCKS, rows, b), (1, 1, 1, GROUPS_PER_BLOCK))
    row_group = lax.broadcasted_iota(jnp.int32, (rows, cols), 0) // a
    col_group = lax.broadcasted_iota(jnp.int32, (rows, cols), 1) // b
    return jnp.where(row_group == col_group, tiled, 0.0)


def _s5_params(A_re, A_im, B_re, B_im, C_re, C_im, log_dt):
    dt = jnp.exp(log_dt)[..., None]
    mag = jnp.exp(A_re * dt)
    ph = A_im * dt
    lb_re, lb_im = mag * jnp.cos(ph), mag * jnp.sin(ph)
    nr, ni = lb_re - 1.0, lb_im
    den = A_re * A_re + A_im * A_im
    f_re = (nr * A_re + ni * A_im) / den
    f_im = (ni * A_re - nr * A_im) / den
    bb_re = f_re[..., None] * B_re - f_im[..., None] * B_im
    bb_im = f_re[..., None] * B_im + f_im[..., None] * B_re
    wb = jnp.concatenate([_block_diag(jnp.swapaxes(bb_re, -1, -2)),
                          _block_diag(jnp.swapaxes(bb_im, -1, -2))], axis=-1)
    wc = jnp.concatenate([_block_diag(jnp.swapaxes(C_re, -1, -2)),
                          _block_diag(-jnp.swapaxes(C_im, -1, -2))], axis=-2)
    lam_shape = (DEPTH, BATCH, SSM_GROUPS * SSM_STATE)
    lr = jnp.broadcast_to(lb_re.reshape(DEPTH, 1, -1), lam_shape)
    li = jnp.broadcast_to(lb_im.reshape(DEPTH, 1, -1), lam_shape)
    return lr, li, wb.astype(_BF16), wc.astype(_BF16)


def kernel(x, p, ffn1_norm, ffn1_w_gate, ffn1_w_up, ffn1_w_down, mix_norm, w_in, conv_w, conv_b, ssm_A_re, ssm_A_im, ssm_B_re, ssm_B_im, ssm_C_re, ssm_C_im, ssm_D, ssm_log_dt, glu_w, glu_b, conv_out_norm, ssm_out_norm, w_out, ffn2_norm, ffn2_w_gate, ffn2_w_up, ffn2_w_down, ple_norm, ple_w_gate, ple_w_proj, final_norm):
    vec = lambda a: a.reshape(DEPTH, 1, -1)

    lr, li, wb, wc = _s5_params(ssm_A_re, ssm_A_im, ssm_B_re, ssm_B_im,
                                ssm_C_re, ssm_C_im, ssm_log_dt)
    mix_small = (vec(mix_norm), conv_w, vec(conv_b), lr, li, wb, wc, vec(ssm_D),
                 vec(glu_b), vec(conv_out_norm), vec(ssm_out_norm))
    ffn1_w = (ffn1_w_gate, ffn1_w_up, ffn1_w_down)
    mix_w = (w_in, glu_w, w_out)
    ffn2_w = (ffn2_w_gate, ffn2_w_up, ffn2_w_down, ple_w_gate, ple_w_proj)
    fin = final_norm.reshape(1, D_MODEL)

    f1_bf = tuple(w[0].astype(_BF16) for w in ffn1_w)
    h = x
    for i in range(DEPTH):
        h, mix_bf = _ffn(h, i, vec(ffn1_norm), f1_bf, x_batch_major=(i == 0),
                         cast=(mix_w, i))
        h, f2_bf = _mix(h, i, mix_small, mix_bf, cast=(ffn2_w, i))
        last = i == DEPTH - 1
        h, f1_bf = _ffn(h, i, vec(ffn2_norm), f2_bf[:3],
                        ple=(p, vec(ple_norm), f2_bf[3], f2_bf[4], fin), final=last,
                        cast=() if last else (ffn1_w, i + 1))
    return h
```

```python
import functools
import math

import jax
import jax.numpy as jnp
from jax import lax
from jax.experimental import pallas as pl
from jax.experimental.pallas import tpu as pltpu

D_MODEL = 1024
BATCH = 8
SEQ = 2048
DEPTH = 4
ROWS = BATCH * SEQ

CONV_WIDTH = 512
CONV_K = 3
SSM_WIDTH = 512
SSM_GROUP = 16
SSM_GROUPS = 32
SSM_STATE = 64
MIX_WIDTH = CONV_WIDTH + SSM_WIDTH
IN_COLS = 3 * CONV_WIDTH + SSM_WIDTH
D_FF = 2816
PLE_DIM = 256
EPS = 1e-6

GROUPS_PER_BLOCK = 16
N_BLOCKS = SSM_GROUPS // GROUPS_PER_BLOCK
BLOCK_IN = GROUPS_PER_BLOCK * SSM_GROUP
BLOCK_STATE = GROUPS_PER_BLOCK * SSM_STATE
STATE_COLS = 2 * SSM_GROUPS * SSM_STATE
SCAN_CHUNK = 512
MIX_PARTS = 2

V7X_LANES = 128
V7X_VMEM_LIMIT_BYTES = 56 * 1024 * 1024
ROW_TILE = 512
FF_CHUNKS = ((0, 1024), (1024, 1024), (2048, 768))
CAST_BLOCKS = 16

_BF16 = jnp.bfloat16
_F32 = jnp.float32


def _rms(x, g):
    return x * lax.rsqrt(jnp.mean(x * x, axis=-1, keepdims=True) + EPS) * g


def _dot(a, b):
    return jnp.dot(a, b, preferred_element_type=_F32)


def _stacked(shape, layer):
    nd = len(shape)
    return pl.BlockSpec((None,) + tuple(shape), lambda i: (layer,) + (0,) * nd,
                        pipeline_mode=pl.Buffered(1))


def _whole(shape):
    nd = len(shape)
    return pl.BlockSpec(tuple(shape), lambda i: (0,) * nd, pipeline_mode=pl.Buffered(1))


def _row_spec(width):
    return pl.BlockSpec((ROW_TILE, width), lambda i: (i, 0))


def _params():
    return pltpu.CompilerParams(dimension_semantics=("arbitrary",),
                                vmem_limit_bytes=V7X_VMEM_LIMIT_BYTES)


def _slab_scratch(width):
    return pltpu.VMEM((width // V7X_LANES, ROW_TILE, V7X_LANES), _F32)


def _rows_from_batch_major(src_ref, slab_ref):
    steps = ROW_TILE // BATCH
    n_slabs = slab_ref.shape[0]
    for b in range(BATCH):
        for c in range(n_slabs):
            slab_ref[c, pl.ds(b, steps, stride=BATCH), :] = (
                src_ref[b, :, c * V7X_LANES:(c + 1) * V7X_LANES])
    return jnp.concatenate([slab_ref[c] for c in range(n_slabs)], axis=-1)


def _rows_to_batch_major(val, slab_ref, dst_ref):
    steps = ROW_TILE // BATCH
    n_slabs = slab_ref.shape[0]
    for c in range(n_slabs):
        slab_ref[c] = val[:, c * V7X_LANES:(c + 1) * V7X_LANES]
    for b in range(BATCH):
        for c in range(n_slabs):
            dst_ref[b, :, c * V7X_LANES:(c + 1) * V7X_LANES] = (
                slab_ref[c, pl.ds(b, steps, stride=BATCH), :])


def _cast_plan(weights, layer):
    grid_steps = ROWS // ROW_TILE
    blocks = min(CAST_BLOCKS, grid_steps)
    per = grid_steps // blocks
    in_specs, out_specs, out_shapes = [], [], []
    for w in weights:
        _, r, c = w.shape
        in_specs.append(pl.BlockSpec((None, r // blocks, c), lambda i: (layer, i // per, 0)))
        out_specs.append(pl.BlockSpec((r // blocks, c), lambda i: (i // per, 0)))
        out_shapes.append(jax.ShapeDtypeStruct((r, c), _BF16))
    return in_specs, out_specs, out_shapes


def _split_refs(refs, n_in, n_cast):
    refs = list(refs)
    main = refs[:n_in]
    cast_in = refs[n_in:n_in + n_cast]
    o_ref = refs[n_in + n_cast]
    cast_out = refs[n_in + n_cast + 1:n_in + 2 * n_cast + 1]
    scratch = refs[n_in + 2 * n_cast + 1:]

    def convert():
        for src, dst in zip(cast_in, cast_out):
            dst[...] = src[...].astype(_BF16)

    return main, o_ref, scratch, convert


def _ffn_body(*refs, n_in, n_cast, x_batch_major, with_ple, final):
    main, o_ref, scratch, convert_next_weights = _split_refs(refs, n_in, n_cast)
    x_ref, g_ref, wg_ref, wu_ref, wd_ref = main[:5]
    slabs = iter(scratch)

    if x_batch_major:
        x = _rows_from_batch_major(x_ref, next(slabs))
    else:
        x = x_ref[...]
    xn = _rms(x, g_ref[...]).astype(_BF16)
    acc = None
    for start, size in FF_CHUNKS:
        gate = _dot(xn, wg_ref[:, start:start + size])
        up = _dot(xn, wu_ref[:, start:start + size])
        act = (gate * jax.nn.sigmoid(gate) * up).astype(_BF16)
        part = _dot(act, wd_ref[start:start + size, :])
        if acc is None:
            acc = part
            convert_next_weights()
        else:
            acc = acc + part
    out = x + 0.5 * acc

    if with_ple:
        p_ref, pn_ref, pg_ref, pp_ref, fn_ref = main[5:]
        p_rows = _rows_from_batch_major(p_ref, next(slabs)).astype(_BF16)
        gate = jax.nn.sigmoid(_dot(_rms(out, pn_ref[...]).astype(_BF16), pg_ref[...]))
        out = out + _dot(p_rows, pp_ref[...]) * gate
    if final:
        _rows_to_batch_major(_rms(out, fn_ref[...]), next(slabs), o_ref)
    else:
        o_ref[...] = out


def _ffn(h, layer, norm, w_bf, *, x_batch_major=False, ple=None, final=False, cast=()):
    batch_major = lambda width: pl.BlockSpec(
        (BATCH, ROW_TILE // BATCH, width), lambda i: (0, i, 0))
    in_specs = [batch_major(D_MODEL) if x_batch_major else _row_spec(D_MODEL),
                _stacked((1, D_MODEL), layer),
                _whole((D_MODEL, D_FF)), _whole((D_MODEL, D_FF)), _whole((D_FF, D_MODEL))]
    args = [h, norm, *w_bf]
    scratch = []
    if x_batch_major:
        scratch.append(_slab_scratch(D_MODEL))
    if ple is not None:
        p, ple_norm, ple_w_gate, ple_w_proj, final_norm = ple
        in_specs += [pl.BlockSpec((None, BATCH, ROW_TILE // BATCH, PLE_DIM),
                                  lambda i: (layer, 0, i, 0)),
                     _stacked((1, D_MODEL), layer),
                     _whole((D_MODEL, D_MODEL)), _whole((PLE_DIM, D_MODEL)),
                     pl.BlockSpec((1, D_MODEL), lambda i: (0, 0))]
        args += [p, ple_norm, ple_w_gate, ple_w_proj, final_norm]
        scratch.append(_slab_scratch(PLE_DIM))
    if final:
        scratch.append(_slab_scratch(D_MODEL))
        out_spec = batch_major(D_MODEL)
        out_shape = jax.ShapeDtypeStruct((BATCH, SEQ, D_MODEL), _F32)
    else:
        out_spec = _row_spec(D_MODEL)
        out_shape = jax.ShapeDtypeStruct((ROWS, D_MODEL), _F32)
    cast_w, cast_layer = cast if cast else ((), 0)
    c_in, c_out, c_shapes = _cast_plan(cast_w, cast_layer)
    outs = pl.pallas_call(
        functools.partial(_ffn_body, n_in=len(args), n_cast=len(cast_w),
                          x_batch_major=x_batch_major, with_ple=ple is not None, final=final),
        grid=(ROWS // ROW_TILE,),
        in_specs=in_specs + c_in,
        out_specs=[out_spec] + c_out,
        out_shape=[out_shape] + c_shapes,
        scratch_shapes=scratch,
        compiler_params=_params(),
        name="ffn_ple" if ple is not None else "ffn",
    )(*args, *cast_w)
    return outs[0], tuple(outs[1:])


def _mix_body(*refs, n_in, n_cast):
    main, o_ref, scratch, convert_next_weights = _split_refs(refs, n_in, n_cast)
    (x_ref, g_ref, win_ref, cw_ref, cb_ref, lr_ref, li_ref, wb_ref, wc_ref,
     d_ref, gw_ref, gb_ref, cn_ref, sn_ref, wo_ref) = main
    state_ref, vbuf_ref, us_ref, bu_ref, hs_ref, ycat_ref = scratch
    part_rows = ROW_TILE // MIX_PARTS
    halo = (CONV_K - 1) * BATCH

    @pl.when(pl.program_id(0) == 0)
    def _():
        state_ref[...] = jnp.zeros_like(state_ref)
        vbuf_ref[0:halo, :] = jnp.zeros((halo, CONV_WIDTH), _F32)

    def in_proj(r0):
        rows = slice(r0, r0 + part_rows)
        xn = _rms(x_ref[rows, :], g_ref[...]).astype(_BF16)
        z = _dot(xn, win_ref[...])
        z_b = z[:, 0:CONV_WIDTH]
        z_c = z[:, CONV_WIDTH:2 * CONV_WIDTH]
        z_v = z[:, 2 * CONV_WIDTH:3 * CONV_WIDTH]
        z_s = z[:, 3 * CONV_WIDTH:]
        vbuf_ref[halo + r0:halo + r0 + part_rows, :] = z_c * z_v
        conv = (cw_ref[2:3, :] * vbuf_ref[2 * BATCH + r0:2 * BATCH + r0 + part_rows, :]
                + cw_ref[1:2, :] * vbuf_ref[BATCH + r0:BATCH + r0 + part_rows, :]
                + cw_ref[0:1, :] * vbuf_ref[r0:r0 + part_rows, :])
        y_a = z_b * (conv + cb_ref[...])
        ycat_ref[rows, 0:CONV_WIDTH] = _rms(y_a, cn_ref[...]).astype(_BF16)
        us_ref[rows, :] = z_s
        u_bf = z_s.astype(_BF16)
        for j in range(N_BLOCKS):
            bu_ref[rows, 2 * BLOCK_STATE * j:2 * BLOCK_STATE * (j + 1)] = _dot(
                u_bf[:, BLOCK_IN * j:BLOCK_IN * (j + 1)], wb_ref[j])

    def scan(r0):
        for j in range(N_BLOCKS):
            for c in range(BLOCK_STATE // SCAN_CHUNK):
                re = slice(2 * BLOCK_STATE * j + SCAN_CHUNK * c,
                           2 * BLOCK_STATE * j + SCAN_CHUNK * (c + 1))
                im = slice(re.start + BLOCK_STATE, re.stop + BLOCK_STATE)
                lam = slice(BLOCK_STATE * j + SCAN_CHUNK * c, BLOCK_STATE * j + SCAN_CHUNK * (c + 1))
                lr, li = lr_ref[:, lam], li_ref[:, lam]
                hr, hi = state_ref[:, re], state_ref[:, im]
                for s in range(part_rows // BATCH):
                    t = slice(r0 + s * BATCH, r0 + (s + 1) * BATCH)
                    hr, hi = (lr * hr - li * hi + bu_ref[t, re],
                              lr * hi + li * hr + bu_ref[t, im])
                    hs_ref[t, re] = hr
                    hs_ref[t, im] = hi
                state_ref[:, re] = hr
                state_ref[:, im] = hi

    def out_proj(r0):
        rows = slice(r0, r0 + part_rows)
        parts = []
        for j in range(N_BLOCKS):
            hs = hs_ref[rows, 2 * BLOCK_STATE * j:2 * BLOCK_STATE * (j + 1)]
            parts.append(_dot(hs.astype(_BF16), wc_ref[j]))
        y = jnp.concatenate(parts, axis=-1) + d_ref[...] * us_ref[rows, :]
        c0 = math.sqrt(2.0 / math.pi)
        zg = 0.5 * y * (1.0 + jnp.tanh(c0 * (y + 0.044715 * (y * y * y))))
        gl = _dot(zg.astype(_BF16), gw_ref[...]) + gb_ref[...]
        y_s = zg * jax.nn.sigmoid(gl)
        ycat_ref[rows, CONV_WIDTH:] = _rms(y_s, sn_ref[...]).astype(_BF16)
        o_ref[rows, :] = x_ref[rows, :] + _dot(ycat_ref[rows, :], wo_ref[...])

    in_proj(0)
    convert_next_weights()
    for part in range(MIX_PARTS):
        if part + 1 < MIX_PARTS:
            in_proj((part + 1) * part_rows)
        scan(part * part_rows)
        out_proj(part * part_rows)
    vbuf_ref[0:halo, :] = vbuf_ref[ROW_TILE:ROW_TILE + halo, :]


def _mix(h, layer, stacked, w_bf, cast):
    (mix_norm, conv_w, conv_b, lr, li, wb, wc, ssm_d, glu_b, conv_norm, ssm_norm) = stacked
    w_in, glu_w, w_out = w_bf
    st = lambda a: _stacked(a.shape[1:], layer)
    wh = lambda a: _whole(a.shape)
    args = [h, mix_norm, w_in, conv_w, conv_b, lr, li, wb, wc, ssm_d, glu_w, glu_b,
            conv_norm, ssm_norm, w_out]
    in_specs = [_row_spec(D_MODEL), st(mix_norm), wh(w_in), st(conv_w), st(conv_b), st(lr),
                st(li), st(wb), st(wc), st(ssm_d), wh(glu_w), st(glu_b), st(conv_norm),
                st(ssm_norm), wh(w_out)]
    cast_w, cast_layer = cast
    c_in, c_out, c_shapes = _cast_plan(cast_w, cast_layer)
    outs = pl.pallas_call(
        functools.partial(_mix_body, n_in=len(args), n_cast=len(cast_w)),
        grid=(ROWS // ROW_TILE,),
        in_specs=in_specs + c_in,
        out_specs=[_row_spec(D_MODEL)] + c_out,
        out_shape=[jax.ShapeDtypeStruct((ROWS, D_MODEL), _F32)] + c_shapes,
        scratch_shapes=[
            pltpu.VMEM((BATCH, STATE_COLS), _F32),
            pltpu.VMEM((ROW_TILE + (CONV_K - 1) * BATCH, CONV_WIDTH), _F32),
            pltpu.VMEM((ROW_TILE, SSM_WIDTH), _F32),
            pltpu.VMEM((ROW_TILE, STATE_COLS), _F32),
            pltpu.VMEM((ROW_TILE, STATE_COLS), _F32),
            pltpu.VMEM((ROW_TILE, MIX_WIDTH), _BF16),
        ],
        compiler_params=_params(),
        name="mix",
    )(*args, *cast_w)
    return outs[0], tuple(outs[1:])


def _block_diag(w):
    d, _, a, b = w.shape
    rows, cols = GROUPS_PER_BLOCK * a, GROUPS_PER_BLOCK * b
    tiled = jnp.tile(w.reshape(d, N_BLOCKS, rows, b), (1, 1, 1, GROUPS_PER_BLOCK))
    row_group = lax.broadcasted_iota(jnp.int32, (rows, cols), 0) // a
    col_group = lax.broadcasted_iota(jnp.int32, (rows, cols), 1) // b
    return jnp.where(row_group == col_group, tiled, 0.0)


def _s5_params(A_re, A_im, B_re, B_im, C_re, C_im, log_dt):
    dt = jnp.exp(log_dt)[..., None]
    mag = jnp.exp(A_re * dt)
    ph = A_im * dt
    lb_re, lb_im = mag * jnp.cos(ph), mag * jnp.sin(ph)
    nr, ni = lb_re - 1.0, lb_im
    den = A_re * A_re + A_im * A_im
    f_re = (nr * A_re + ni * A_im) / den
    f_im = (ni * A_re - nr * A_im) / den
    bb_re = f_re[..., None] * B_re - f_im[..., None] * B_im
    bb_im = f_re[..., None] * B_im + f_im[..., None] * B_re
    wb = jnp.concatenate([_block_diag(jnp.swapaxes(bb_re, -1, -2)),
                          _block_diag(jnp.swapaxes(bb_im, -1, -2))], axis=-1)
    wc = jnp.concatenate([_block_diag(jnp.swapaxes(C_re, -1, -2)),
                          _block_diag(-jnp.swapaxes(C_im, -1, -2))], axis=-2)
    lam_shape = (DEPTH, BATCH, SSM_GROUPS * SSM_STATE)
    lr = jnp.broadcast_to(lb_re.reshape(DEPTH, 1, -1), lam_shape)
    li = jnp.broadcast_to(lb_im.reshape(DEPTH, 1, -1), lam_shape)
    return lr, li, wb.astype(_BF16), wc.astype(_BF16)


def kernel(x, p, ffn1_norm, ffn1_w_gate, ffn1_w_up, ffn1_w_down, mix_norm, w_in, conv_w, conv_b, ssm_A_re, ssm_A_im, ssm_B_re, ssm_B_im, ssm_C_re, ssm_C_im, ssm_D, ssm_log_dt, glu_w, glu_b, conv_out_norm, ssm_out_norm, w_out, ffn2_norm, ffn2_w_gate, ffn2_w_up, ffn2_w_down, ple_norm, ple_w_gate, ple_w_proj, final_norm):
    vec = lambda a: a.reshape(DEPTH, 1, -1)

    lr, li, wb, wc = _s5_params(ssm_A_re, ssm_A_im, ssm_B_re, ssm_B_im,
                                ssm_C_re, ssm_C_im, ssm_log_dt)
    mix_small = (vec(mix_norm), conv_w, vec(conv_b), lr, li, wb, wc, vec(ssm_D),
                 vec(glu_b), vec(conv_out_norm), vec(ssm_out_norm))
    ffn1_w = (ffn1_w_gate, ffn1_w_up, ffn1_w_down)
    mix_w = (w_in, glu_w, w_out)
    ffn2_w = (ffn2_w_gate, ffn2_w_up, ffn2_w_down, ple_w_gate, ple_w_proj)
    fin = final_norm.reshape(1, D_MODEL)

    f1_bf = tuple(w[0].astype(_BF16) for w in ffn1_w)
    h = x
    for i in range(DEPTH):
        h, mix_bf = _ffn(h, i, vec(ffn1_norm), f1_bf, x_batch_major=(i == 0),
                         cast=(mix_w, i))
        h, f2_bf = _mix(h, i, mix_small, mix_bf, cast=(ffn2_w, i))
        last = i == DEPTH - 1
        h, f1_bf = _ffn(h, i, vec(ffn2_norm), f2_bf[:3],
                        ple=(p, vec(ple_norm), f2_bf[3], f2_bf[4], fin), final=last,
                        cast=() if last else (ffn1_w, i + 1))
    return h
```

```python
import functools
import math

import jax
import jax.numpy as jnp
from jax import lax
from jax.experimental import pallas as pl
from jax.experimental.pallas import tpu as pltpu

D_MODEL = 1024
BATCH = 8
SEQ = 2048
DEPTH = 4
ROWS = BATCH * SEQ

CONV_WIDTH = 512
CONV_K = 3
SSM_WIDTH = 512
SSM_GROUP = 16
SSM_GROUPS = 32
SSM_STATE = 64
MIX_WIDTH = CONV_WIDTH + SSM_WIDTH
IN_COLS = 3 * CONV_WIDTH + SSM_WIDTH
D_FF = 2816
PLE_DIM = 256
EPS = 1e-6

GROUPS_PER_BLOCK = 16
N_BLOCKS = SSM_GROUPS // GROUPS_PER_BLOCK
BLOCK_IN = GROUPS_PER_BLOCK * SSM_GROUP
BLOCK_STATE = GROUPS_PER_BLOCK * SSM_STATE
STATE_COLS = 2 * SSM_GROUPS * SSM_STATE
SCAN_CHUNK = 512
FFN_PART_ROWS = 512
MIX_PARTS = 2

V7X_LANES = 128
V7X_VMEM_LIMIT_BYTES = 56 * 1024 * 1024
ROW_TILE = 512
FFN_ROW_TILE = 1024
FF_CHUNK_ELEMS = 512 * 1024
CAST_BLOCKS = 16

_BF16 = jnp.bfloat16
_F32 = jnp.float32


def _rms(x, g):
    return x * lax.rsqrt(jnp.mean(x * x, axis=-1, keepdims=True) + EPS) * g


def _dot(a, b):
    return jnp.dot(a, b, preferred_element_type=_F32)


def _stacked(shape, layer):
    nd = len(shape)
    return pl.BlockSpec((None,) + tuple(shape), lambda i: (layer,) + (0,) * nd,
                        pipeline_mode=pl.Buffered(1))


def _whole(shape):
    nd = len(shape)
    return pl.BlockSpec(tuple(shape), lambda i: (0,) * nd, pipeline_mode=pl.Buffered(1))


def _row_spec(width, row_tile):
    return pl.BlockSpec((row_tile, width), lambda i: (i, 0))


def _params():
    return pltpu.CompilerParams(dimension_semantics=("arbitrary",),
                                vmem_limit_bytes=V7X_VMEM_LIMIT_BYTES)


def _slab_scratch(width, row_tile):
    return pltpu.VMEM((width // V7X_LANES, row_tile, V7X_LANES), _F32)


def _rows_from_batch_major(src_ref, slab_ref):
    n_slabs, row_tile, _ = slab_ref.shape
    steps = row_tile // BATCH
    for b in range(BATCH):
        for c in range(n_slabs):
            slab_ref[c, pl.ds(b, steps, stride=BATCH), :] = (
                src_ref[b, :, c * V7X_LANES:(c + 1) * V7X_LANES])
    return jnp.concatenate([slab_ref[c] for c in range(n_slabs)], axis=-1)


def _rows_to_batch_major(val, slab_ref, dst_ref):
    n_slabs, row_tile, _ = slab_ref.shape
    steps = row_tile // BATCH
    for c in range(n_slabs):
        slab_ref[c] = val[:, c * V7X_LANES:(c + 1) * V7X_LANES]
    for b in range(BATCH):
        for c in range(n_slabs):
            dst_ref[b, :, c * V7X_LANES:(c + 1) * V7X_LANES] = (
                slab_ref[c, pl.ds(b, steps, stride=BATCH), :])


def _cast_plan(weights, layer, grid_steps):
    blocks = min(CAST_BLOCKS, grid_steps)
    per = grid_steps // blocks
    in_specs, out_specs, out_shapes = [], [], []
    for w in weights:
        _, r, c = w.shape
        in_specs.append(pl.BlockSpec((None, r // blocks, c), lambda i: (layer, i // per, 0)))
        out_specs.append(pl.BlockSpec((r // blocks, c), lambda i: (i // per, 0)))
        out_shapes.append(jax.ShapeDtypeStruct((r, c), _BF16))
    return in_specs, out_specs, out_shapes


def _split_refs(refs, n_in, n_cast):
    refs = list(refs)
    main = refs[:n_in]
    cast_in = refs[n_in:n_in + n_cast]
    o_ref = refs[n_in + n_cast]
    cast_out = refs[n_in + n_cast + 1:n_in + 2 * n_cast + 1]
    scratch = refs[n_in + 2 * n_cast + 1:]

    def convert():
        for src, dst in zip(cast_in, cast_out):
            dst[...] = src[...].astype(_BF16)

    return main, o_ref, scratch, convert


def _ffn_body(*refs, n_in, n_cast, x_batch_major, with_ple, final):
    main, o_ref, scratch, convert_next_weights = _split_refs(refs, n_in, n_cast)
    x_ref, g_ref, wg_ref, wu_ref, wd_ref = main[:5]
    slabs = iter(scratch)

    if x_batch_major:
        x = _rows_from_batch_major(x_ref, next(slabs))
    else:
        x = x_ref[...]
    if with_ple:
        p_ref, pn_ref, pg_ref, pp_ref, fn_ref = main[5:]
        p_rows = _rows_from_batch_major(p_ref, next(slabs)).astype(_BF16)
    convert_next_weights()

    chunk = FF_CHUNK_ELEMS // FFN_PART_ROWS
    outs = []
    for r0 in range(0, x.shape[0], FFN_PART_ROWS):
        rows = slice(r0, r0 + FFN_PART_ROWS)
        xp = x[rows]
        xn = _rms(xp, g_ref[...]).astype(_BF16)
        acc = None
        for start in range(0, D_FF, chunk):
            size = min(chunk, D_FF - start)
            gate = _dot(xn, wg_ref[:, start:start + size])
            up = _dot(xn, wu_ref[:, start:start + size])
            act = (gate * jax.nn.sigmoid(gate) * up).astype(_BF16)
            down = _dot(act, wd_ref[start:start + size, :])
            acc = down if acc is None else acc + down
        out = xp + 0.5 * acc
        if with_ple:
            gate = jax.nn.sigmoid(_dot(_rms(out, pn_ref[...]).astype(_BF16), pg_ref[...]))
            out = out + _dot(p_rows[rows], pp_ref[...]) * gate
        outs.append(out)
    out = jnp.concatenate(outs, axis=0)
    if final:
        _rows_to_batch_major(_rms(out, fn_ref[...]), next(slabs), o_ref)
    else:
        o_ref[...] = out


def _ffn(h, layer, norm, w_bf, *, row_tile, x_batch_major=False, ple=None, final=False, cast=()):
    batch_major = lambda width: pl.BlockSpec(
        (BATCH, row_tile // BATCH, width), lambda i: (0, i, 0))
    in_specs = [batch_major(D_MODEL) if x_batch_major else _row_spec(D_MODEL, row_tile),
                _stacked((1, D_MODEL), layer),
                _whole((D_MODEL, D_FF)), _whole((D_MODEL, D_FF)), _whole((D_FF, D_MODEL))]
    args = [h, norm, *w_bf]
    scratch = []
    if x_batch_major:
        scratch.append(_slab_scratch(D_MODEL, row_tile))
    if ple is not None:
        p, ple_norm, ple_w_gate, ple_w_proj, final_norm = ple
        in_specs += [pl.BlockSpec((None, BATCH, row_tile // BATCH, PLE_DIM),
                                  lambda i: (layer, 0, i, 0)),
                     _stacked((1, D_MODEL), layer),
                     _whole((D_MODEL, D_MODEL)), _whole((PLE_DIM, D_MODEL)),
                     pl.BlockSpec((1, D_MODEL), lambda i: (0, 0))]
        args += [p, ple_norm, ple_w_gate, ple_w_proj, final_norm]
        scratch.append(_slab_scratch(PLE_DIM, row_tile))
    if final:
        scratch.append(_slab_scratch(D_MODEL, row_tile))
        out_spec = batch_major(D_MODEL)
        out_shape = jax.ShapeDtypeStruct((BATCH, SEQ, D_MODEL), _F32)
    else:
        out_spec = _row_spec(D_MODEL, row_tile)
        out_shape = jax.ShapeDtypeStruct((ROWS, D_MODEL), _F32)
    cast_w, cast_layer = cast if cast else ((), 0)
    c_in, c_out, c_shapes = _cast_plan(cast_w, cast_layer, ROWS // row_tile)
    outs = pl.pallas_call(
        functools.partial(_ffn_body, n_in=len(args), n_cast=len(cast_w),
                          x_batch_major=x_batch_major, with_ple=ple is not None, final=final),
        grid=(ROWS // row_tile,),
        in_specs=in_specs + c_in,
        out_specs=[out_spec] + c_out,
        out_shape=[out_shape] + c_shapes,
        scratch_shapes=scratch,
        compiler_params=_params(),
        name="ffn_ple" if ple is not None else "ffn",
    )(*args, *cast_w)
    return outs[0], tuple(outs[1:])


def _mix_body(*refs, n_in, n_cast):
    main, o_ref, scratch, convert_next_weights = _split_refs(refs, n_in, n_cast)
    (x_ref, g_ref, win_ref, cw_ref, cb_ref, lr_ref, li_ref, wb_ref, wc_ref,
     d_ref, gw_ref, gb_ref, cn_ref, sn_ref, wo_ref) = main
    state_ref, vbuf_ref, us_ref, bu_ref, hs_ref, ycat_ref = scratch
    part_rows = ROW_TILE // MIX_PARTS
    halo = (CONV_K - 1) * BATCH

    @pl.when(pl.program_id(0) == 0)
    def _():
        state_ref[...] = jnp.zeros_like(state_ref)
        vbuf_ref[0:halo, :] = jnp.zeros((halo, CONV_WIDTH), _F32)

    def in_proj(r0):
        rows = slice(r0, r0 + part_rows)
        xn = _rms(x_ref[rows, :], g_ref[...]).astype(_BF16)
        z = _dot(xn, win_ref[...])
        z_b = z[:, 0:CONV_WIDTH]
        z_c = z[:, CONV_WIDTH:2 * CONV_WIDTH]
        z_v = z[:, 2 * CONV_WIDTH:3 * CONV_WIDTH]
        z_s = z[:, 3 * CONV_WIDTH:]
        vbuf_ref[halo + r0:halo + r0 + part_rows, :] = z_c * z_v
        conv = (cw_ref[2:3, :] * vbuf_ref[2 * BATCH + r0:2 * BATCH + r0 + part_rows, :]
                + cw_ref[1:2, :] * vbuf_ref[BATCH + r0:BATCH + r0 + part_rows, :]
                + cw_ref[0:1, :] * vbuf_ref[r0:r0 + part_rows, :])
        y_a = z_b * (conv + cb_ref[...])
        ycat_ref[rows, 0:CONV_WIDTH] = _rms(y_a, cn_ref[...]).astype(_BF16)
        us_ref[rows, :] = z_s
        u_bf = z_s.astype(_BF16)
        for j in range(N_BLOCKS):
            bu_ref[rows, 2 * BLOCK_STATE * j:2 * BLOCK_STATE * (j + 1)] = _dot(
                u_bf[:, BLOCK_IN * j:BLOCK_IN * (j + 1)], wb_ref[j])

    def scan(r0):
        for j in range(N_BLOCKS):
            for c in range(BLOCK_STATE // SCAN_CHUNK):
                re = slice(2 * BLOCK_STATE * j + SCAN_CHUNK * c,
                           2 * BLOCK_STATE * j + SCAN_CHUNK * (c + 1))
                im = slice(re.start + BLOCK_STATE, re.stop + BLOCK_STATE)
                lam = slice(BLOCK_STATE * j + SCAN_CHUNK * c, BLOCK_STATE * j + SCAN_CHUNK * (c + 1))
                lr, li = lr_ref[:, lam], li_ref[:, lam]
                hr, hi = state_ref[:, re], state_ref[:, im]
                for s in range(0, part_rows // BATCH, 2):
                    pair = []
                    for t0 in (r0 + s * BATCH, r0 + (s + 1) * BATCH):
                        t = slice(t0, t0 + BATCH)
                        hr, hi = (lr * hr - li * hi + bu_ref[t, re],
                                  lr * hi + li * hr + bu_ref[t, im])
                        pair.append((hr, hi))
                    t2 = slice(r0 + s * BATCH, r0 + (s + 2) * BATCH)
                    hs_ref[t2, re] = jnp.concatenate([pair[0][0], pair[1][0]], axis=0).astype(_BF16)
                    hs_ref[t2, im] = jnp.concatenate([pair[0][1], pair[1][1]], axis=0).astype(_BF16)
                state_ref[:, re] = hr
                state_ref[:, im] = hi

    def out_proj(r0):
        rows = slice(r0, r0 + part_rows)
        parts = []
        for j in range(N_BLOCKS):
            hs = hs_ref[rows, 2 * BLOCK_STATE * j:2 * BLOCK_STATE * (j + 1)]
            parts.append(_dot(hs, wc_ref[j]))
        y = jnp.concatenate(parts, axis=-1) + d_ref[...] * us_ref[rows, :]
        c0 = math.sqrt(2.0 / math.pi)
        zg = 0.5 * y * (1.0 + jnp.tanh(c0 * (y + 0.044715 * (y * y * y))))
        gl = _dot(zg.astype(_BF16), gw_ref[...]) + gb_ref[...]
        y_s = zg * jax.nn.sigmoid(gl)
        ycat_ref[rows, CONV_WIDTH:] = _rms(y_s, sn_ref[...]).astype(_BF16)
        o_ref[rows, :] = x_ref[rows, :] + _dot(ycat_ref[rows, :], wo_ref[...])

    in_proj(0)
    convert_next_weights()
    for part in range(MIX_PARTS):
        if part + 1 < MIX_PARTS:
            in_proj((part + 1) * part_rows)
        scan(part * part_rows)
        out_proj(part * part_rows)
    vbuf_ref[0:halo, :] = vbuf_ref[ROW_TILE:ROW_TILE + halo, :]


def _mix(h, layer, stacked, w_bf, cast):
    (mix_norm, conv_w, conv_b, lr, li, wb, wc, ssm_d, glu_b, conv_norm, ssm_norm) = stacked
    w_in, glu_w, w_out = w_bf
    st = lambda a: _stacked(a.shape[1:], layer)
    wh = lambda a: _whole(a.shape)
    args = [h, mix_norm, w_in, conv_w, conv_b, lr, li, wb, wc, ssm_d, glu_w, glu_b,
            conv_norm, ssm_norm, w_out]
    in_specs = [_row_spec(D_MODEL, ROW_TILE), st(mix_norm), wh(w_in), st(conv_w), st(conv_b), st(lr),
                st(li), st(wb), st(wc), st(ssm_d), wh(glu_w), st(glu_b), st(conv_norm),
                st(ssm_norm), wh(w_out)]
    cast_w, cast_layer = cast
    c_in, c_out, c_shapes = _cast_plan(cast_w, cast_layer, ROWS // ROW_TILE)
    outs = pl.pallas_call(
        functools.partial(_mix_body, n_in=len(args), n_cast=len(cast_w)),
        grid=(ROWS // ROW_TILE,),
        in_specs=in_specs + c_in,
        out_specs=[_row_spec(D_MODEL, ROW_TILE)] + c_out,
        out_shape=[jax.ShapeDtypeStruct((ROWS, D_MODEL), _F32)] + c_shapes,
        scratch_shapes=[
            pltpu.VMEM((BATCH, STATE_COLS), _F32),
            pltpu.VMEM((ROW_TILE + (CONV_K - 1) * BATCH, CONV_WIDTH), _F32),
            pltpu.VMEM((ROW_TILE, SSM_WIDTH), _F32),
            pltpu.VMEM((ROW_TILE, STATE_COLS), _F32),
            pltpu.VMEM((ROW_TILE, STATE_COLS), _BF16),
            pltpu.VMEM((ROW_TILE, MIX_WIDTH), _BF16),
        ],
        compiler_params=_params(),
        name="mix",
    )(*args, *cast_w)
    return outs[0], tuple(outs[1:])


def _block_diag(w):
    d, _, a, b = w.shape
    rows, cols = GROUPS_PER_BLOCK * a, GROUPS_PER_BLOCK * b
    tiled = jnp.tile(w.reshape(d, N_BLOCKS, rows, b), (1, 1, 1, GROUPS_PER_BLOCK))
    row_group = lax.broadcasted_iota(jnp.int32, (rows, cols), 0) // a
    col_group = lax.broadcasted_iota(jnp.int32, (rows, cols), 1) // b
    return jnp.where(row_group == col_group, tiled, 0.0)


def _s5_params(A_re, A_im, B_re, B_im, C_re, C_im, log_dt):
    dt = jnp.exp(log_dt)[..., None]
    mag = jnp.exp(A_re * dt)
    ph = A_im * dt
    lb_re, lb_im = mag * jnp.cos(ph), mag * jnp.sin(ph)
    nr, ni = lb_re - 1.0, lb_im
    den = A_re * A_re + A_im * A_im
    f_re = (nr * A_re + ni * A_im) / den
    f_im = (ni * A_re - nr * A_im) / den
    bb_re = f_re[..., None] * B_re - f_im[..., None] * B_im
    bb_im = f_re[..., None] * B_im + f_im[..., None] * B_re
    wb = jnp.concatenate([_block_diag(jnp.swapaxes(bb_re, -1, -2)),
                          _block_diag(jnp.swapaxes(bb_im, -1, -2))], axis=-1)
    wc = jnp.concatenate([_block_diag(jnp.swapaxes(C_re, -1, -2)),
                          _block_diag(-jnp.swapaxes(C_im, -1, -2))], axis=-2)
    lam_shape = (DEPTH, BATCH, SSM_GROUPS * SSM_STATE)
    lr = jnp.broadcast_to(lb_re.reshape(DEPTH, 1, -1), lam_shape)
    li = jnp.broadcast_to(lb_im.reshape(DEPTH, 1, -1), lam_shape)
    return lr, li, wb.astype(_BF16), wc.astype(_BF16)


def kernel(x, p, ffn1_norm, ffn1_w_gate, ffn1_w_up, ffn1_w_down, mix_norm, w_in, conv_w, conv_b, ssm_A_re, ssm_A_im, ssm_B_re, ssm_B_im, ssm_C_re, ssm_C_im, ssm_D, ssm_log_dt, glu_w, glu_b, conv_out_norm, ssm_out_norm, w_out, ffn2_norm, ffn2_w_gate, ffn2_w_up, ffn2_w_down, ple_norm, ple_w_gate, ple_w_proj, final_norm):
    vec = lambda a: a.reshape(DEPTH, 1, -1)

    lr, li, wb, wc = _s5_params(ssm_A_re, ssm_A_im, ssm_B_re, ssm_B_im,
                                ssm_C_re, ssm_C_im, ssm_log_dt)
    mix_small = (vec(mix_norm), conv_w, vec(conv_b), lr, li, wb, wc, vec(ssm_D),
                 vec(glu_b), vec(conv_out_norm), vec(ssm_out_norm))
    ffn1_w = (ffn1_w_gate, ffn1_w_up, ffn1_w_down)
    mix_w = (w_in, glu_w, w_out)
    ffn2_w = (ffn2_w_gate, ffn2_w_up, ffn2_w_down, ple_w_gate, ple_w_proj)
    fin = final_norm.reshape(1, D_MODEL)

    f1_bf = tuple(w[0].astype(_BF16) for w in ffn1_w)
    h = x
    for i in range(DEPTH):
        h, mix_bf = _ffn(h, i, vec(ffn1_norm), f1_bf, row_tile=FFN_ROW_TILE,
                         x_batch_major=(i == 0), cast=(mix_w, i))
        h, f2_bf = _mix(h, i, mix_small, mix_bf, cast=(ffn2_w, i))
        last = i == DEPTH - 1
        h, f1_bf = _ffn(h, i, vec(ffn2_norm), f2_bf[:3], row_tile=ROW_TILE,
                        ple=(p, vec(ple_norm), f2_bf[3], f2_bf[4], fin), final=last,
                        cast=() if last else (ffn1_w, i + 1))
    return h
```

```python
import functools
import math

import jax
import jax.numpy as jnp
from jax import lax
from jax.experimental import pallas as pl
from jax.experimental.pallas import tpu as pltpu

D_MODEL = 1024
BATCH = 8
SEQ = 2048
DEPTH = 4
ROWS = BATCH * SEQ

CONV_WIDTH = 512
CONV_K = 3
SSM_WIDTH = 512
SSM_GROUP = 16
SSM_GROUPS = 32
SSM_STATE = 64
MIX_WIDTH = CONV_WIDTH + SSM_WIDTH
IN_COLS = 3 * CONV_WIDTH + SSM_WIDTH
D_FF = 2816
PLE_DIM = 256
EPS = 1e-6

GROUPS_PER_BLOCK = 16
N_BLOCKS = SSM_GROUPS // GROUPS_PER_BLOCK
BLOCK_IN = GROUPS_PER_BLOCK * SSM_GROUP
BLOCK_STATE = GROUPS_PER_BLOCK * SSM_STATE
STATE_COLS = 2 * SSM_GROUPS * SSM_STATE
SCAN_CHUNK = 512
FFN_PARTS = 2
MIX_PARTS = 2

V7X_LANES = 128
V7X_VMEM_LIMIT_BYTES = 56 * 1024 * 1024
ROW_TILE = 512
FFN_ROW_TILE = 1024
FF_CHUNK_ELEMS = 512 * 1024
CAST_BLOCKS = 16

_BF16 = jnp.bfloat16
_F32 = jnp.float32


def _rms(x, g):
    return x * lax.rsqrt(jnp.mean(x * x, axis=-1, keepdims=True) + EPS) * g


def _dot(a, b):
    return jnp.dot(a, b, preferred_element_type=_F32)


def _stacked(shape, layer):
    nd = len(shape)
    return pl.BlockSpec((None,) + tuple(shape), lambda i: (layer,) + (0,) * nd,
                        pipeline_mode=pl.Buffered(1))


def _whole(shape):
    nd = len(shape)
    return pl.BlockSpec(tuple(shape), lambda i: (0,) * nd, pipeline_mode=pl.Buffered(1))


def _row_spec(width, row_tile):
    return pl.BlockSpec((row_tile, width), lambda i: (i, 0))


def _params():
    return pltpu.CompilerParams(dimension_semantics=("arbitrary",),
                                vmem_limit_bytes=V7X_VMEM_LIMIT_BYTES)


def _slab_scratch(width, row_tile):
    return pltpu.VMEM((width // V7X_LANES, row_tile, V7X_LANES), _F32)


def _rows_from_batch_major(src_ref, slab_ref):
    n_slabs, row_tile, _ = slab_ref.shape
    steps = row_tile // BATCH
    for b in range(BATCH):
        for c in range(n_slabs):
            slab_ref[c, pl.ds(b, steps, stride=BATCH), :] = (
                src_ref[b, :, c * V7X_LANES:(c + 1) * V7X_LANES])
    return jnp.concatenate([slab_ref[c] for c in range(n_slabs)], axis=-1)


def _rows_to_batch_major(val, slab_ref, dst_ref):
    n_slabs, row_tile, _ = slab_ref.shape
    steps = row_tile // BATCH
    for c in range(n_slabs):
        slab_ref[c] = val[:, c * V7X_LANES:(c + 1) * V7X_LANES]
    for b in range(BATCH):
        for c in range(n_slabs):
            dst_ref[b, :, c * V7X_LANES:(c + 1) * V7X_LANES] = (
                slab_ref[c, pl.ds(b, steps, stride=BATCH), :])


def _cast_plan(weights, layer, grid_steps):
    blocks = min(CAST_BLOCKS, grid_steps)
    per = grid_steps // blocks
    in_specs, out_specs, out_shapes = [], [], []
    for w in weights:
        _, r, c = w.shape
        in_specs.append(pl.BlockSpec((None, r // blocks, c), lambda i: (layer, i // per, 0)))
        out_specs.append(pl.BlockSpec((r // blocks, c), lambda i: (i // per, 0)))
        out_shapes.append(jax.ShapeDtypeStruct((r, c), _BF16))
    return in_specs, out_specs, out_shapes


def _split_refs(refs, n_in, n_cast):
    refs = list(refs)
    main = refs[:n_in]
    cast_in = refs[n_in:n_in + n_cast]
    o_ref = refs[n_in + n_cast]
    cast_out = refs[n_in + n_cast + 1:n_in + 2 * n_cast + 1]
    scratch = refs[n_in + 2 * n_cast + 1:]

    def convert():
        for src, dst in zip(cast_in, cast_out):
            dst[...] = src[...].astype(_BF16)

    return main, o_ref, scratch, convert


def _ffn_body(*refs, n_in, n_cast, x_batch_major, with_ple, final):
    main, o_ref, scratch, convert_next_weights = _split_refs(refs, n_in, n_cast)
    x_ref, g_ref, wg_ref, wu_ref, wd_ref = main[:5]
    slabs = iter(scratch)

    if x_batch_major:
        x = _rows_from_batch_major(x_ref, next(slabs))
    else:
        x = x_ref[...]
    if with_ple:
        p_ref, pn_ref, pg_ref, pp_ref, fn_ref = main[5:]
        p_rows = _rows_from_batch_major(p_ref, next(slabs)).astype(_BF16)
    convert_next_weights()

    part_rows = x.shape[0] // FFN_PARTS
    chunk = FF_CHUNK_ELEMS // part_rows
    parts = [slice(r0, r0 + part_rows) for r0 in range(0, x.shape[0], part_rows)]
    xns = [_rms(x[rows], g_ref[...]).astype(_BF16) for rows in parts]
    items = [(k, start) for k in range(FFN_PARTS) for start in range(0, D_FF, chunk)]
    accs = [None] * FFN_PARTS
    pending = None
    for item in items + [None]:
        if item is not None:
            k, start = item
            cols = slice(start, min(start + chunk, D_FF))
            issued = (k, cols, _dot(xns[k], wg_ref[:, cols]), _dot(xns[k], wu_ref[:, cols]))
        if pending is not None:
            k, cols, gate, up = pending
            act = (gate * jax.nn.sigmoid(gate) * up).astype(_BF16)
            down = _dot(act, wd_ref[cols, :])
            accs[k] = down if accs[k] is None else accs[k] + down
        pending = issued if item is not None else None
    outs = [x[rows] + 0.5 * acc for rows, acc in zip(parts, accs)]
    if with_ple:
        gates = [_dot(_rms(out, pn_ref[...]).astype(_BF16), pg_ref[...]) for out in outs]
        projs = [_dot(p_rows[rows], pp_ref[...]) for rows in parts]
        outs = [out + proj * jax.nn.sigmoid(gate) for out, proj, gate in zip(outs, projs, gates)]
    out = jnp.concatenate(outs, axis=0)
    if final:
        _rows_to_batch_major(_rms(out, fn_ref[...]), next(slabs), o_ref)
    else:
        o_ref[...] = out


def _ffn(h, layer, norm, w_bf, *, row_tile, x_batch_major=False, ple=None, final=False, cast=()):
    batch_major = lambda width: pl.BlockSpec(
        (BATCH, row_tile // BATCH, width), lambda i: (0, i, 0))
    in_specs = [batch_major(D_MODEL) if x_batch_major else _row_spec(D_MODEL, row_tile),
                _stacked((1, D_MODEL), layer),
                _whole((D_MODEL, D_FF)), _whole((D_MODEL, D_FF)), _whole((D_FF, D_MODEL))]
    args = [h, norm, *w_bf]
    scratch = []
    if x_batch_major:
        scratch.append(_slab_scratch(D_MODEL, row_tile))
    if ple is not None:
        p, ple_norm, ple_w_gate, ple_w_proj, final_norm = ple
        in_specs += [pl.BlockSpec((None, BATCH, row_tile // BATCH, PLE_DIM),
                                  lambda i: (layer, 0, i, 0)),
                     _stacked((1, D_MODEL), layer),
                     _whole((D_MODEL, D_MODEL)), _whole((PLE_DIM, D_MODEL)),
                     pl.BlockSpec((1, D_MODEL), lambda i: (0, 0))]
        args += [p, ple_norm, ple_w_gate, ple_w_proj, final_norm]
        scratch.append(_slab_scratch(PLE_DIM, row_tile))
    if final:
        scratch.append(_slab_scratch(D_MODEL, row_tile))
        out_spec = batch_major(D_MODEL)
        out_shape = jax.ShapeDtypeStruct((BATCH, SEQ, D_MODEL), _F32)
    else:
        out_spec = _row_spec(D_MODEL, row_tile)
        out_shape = jax.ShapeDtypeStruct((ROWS, D_MODEL), _F32)
    cast_w, cast_layer = cast if cast else ((), 0)
    c_in, c_out, c_shapes = _cast_plan(cast_w, cast_layer, ROWS // row_tile)
    outs = pl.pallas_call(
        functools.partial(_ffn_body, n_in=len(args), n_cast=len(cast_w),
                          x_batch_major=x_batch_major, with_ple=ple is not None, final=final),
        grid=(ROWS // row_tile,),
        in_specs=in_specs + c_in,
        out_specs=[out_spec] + c_out,
        out_shape=[out_shape] + c_shapes,
        scratch_shapes=scratch,
        compiler_params=_params(),
        name="ffn_ple" if ple is not None else "ffn",
    )(*args, *cast_w)
    return outs[0], tuple(outs[1:])


def _mix_body(*refs, n_in, n_cast):
    main, o_ref, scratch, convert_next_weights = _split_refs(refs, n_in, n_cast)
    (x_ref, g_ref, win_ref, cw_ref, cb_ref, lr_ref, li_ref, wb_ref, wc_ref,
     d_ref, gw_ref, gb_ref, cn_ref, sn_ref, wo_ref) = main
    state_ref, vbuf_ref, us_ref, bu_ref, hs_ref, ycat_ref = scratch
    part_rows = ROW_TILE // MIX_PARTS
    halo = (CONV_K - 1) * BATCH

    @pl.when(pl.program_id(0) == 0)
    def _():
        state_ref[...] = jnp.zeros_like(state_ref)
        vbuf_ref[0:halo, :] = jnp.zeros((halo, CONV_WIDTH), _F32)

    def in_proj(r0):
        rows = slice(r0, r0 + part_rows)
        xn = _rms(x_ref[rows, :], g_ref[...]).astype(_BF16)
        z = _dot(xn, win_ref[...])
        z_b = z[:, 0:CONV_WIDTH]
        z_c = z[:, CONV_WIDTH:2 * CONV_WIDTH]
        z_v = z[:, 2 * CONV_WIDTH:3 * CONV_WIDTH]
        z_s = z[:, 3 * CONV_WIDTH:]
        vbuf_ref[halo + r0:halo + r0 + part_rows, :] = z_c * z_v
        conv = (cw_ref[2:3, :] * vbuf_ref[2 * BATCH + r0:2 * BATCH + r0 + part_rows, :]
                + cw_ref[1:2, :] * vbuf_ref[BATCH + r0:BATCH + r0 + part_rows, :]
                + cw_ref[0:1, :] * vbuf_ref[r0:r0 + part_rows, :])
        y_a = z_b * (conv + cb_ref[...])
        ycat_ref[rows, 0:CONV_WIDTH] = _rms(y_a, cn_ref[...]).astype(_BF16)
        us_ref[rows, :] = z_s
        return z_s.astype(_BF16)

    def state_in(r0, u_bf):
        rows = slice(r0, r0 + part_rows)
        for j in range(N_BLOCKS):
            bu_ref[rows, 2 * BLOCK_STATE * j:2 * BLOCK_STATE * (j + 1)] = _dot(
                u_bf[:, BLOCK_IN * j:BLOCK_IN * (j + 1)], wb_ref[j])

    def scan(r0):
        for j in range(N_BLOCKS):
            for c in range(BLOCK_STATE // SCAN_CHUNK):
                re = slice(2 * BLOCK_STATE * j + SCAN_CHUNK * c,
                           2 * BLOCK_STATE * j + SCAN_CHUNK * (c + 1))
                im = slice(re.start + BLOCK_STATE, re.stop + BLOCK_STATE)
                lam = slice(BLOCK_STATE * j + SCAN_CHUNK * c, BLOCK_STATE * j + SCAN_CHUNK * (c + 1))
                lr, li = lr_ref[:, lam], li_ref[:, lam]
                hr, hi = state_ref[:, re], state_ref[:, im]
                for s in range(0, part_rows // BATCH, 2):
                    pair = []
                    for t0 in (r0 + s * BATCH, r0 + (s + 1) * BATCH):
                        t = slice(t0, t0 + BATCH)
                        hr, hi = (lr * hr - li * hi + bu_ref[t, re],
                                  lr * hi + li * hr + bu_ref[t, im])
                        pair.append((hr, hi))
                    t2 = slice(r0 + s * BATCH, r0 + (s + 2) * BATCH)
                    hs_ref[t2, re] = jnp.concatenate([pair[0][0], pair[1][0]], axis=0).astype(_BF16)
                    hs_ref[t2, im] = jnp.concatenate([pair[0][1], pair[1][1]], axis=0).astype(_BF16)
                state_ref[:, re] = hr
                state_ref[:, im] = hi

    def state_out(r0):
        rows = slice(r0, r0 + part_rows)
        parts = []
        for j in range(N_BLOCKS):
            hs = hs_ref[rows, 2 * BLOCK_STATE * j:2 * BLOCK_STATE * (j + 1)]
            parts.append(_dot(hs, wc_ref[j]))
        y = jnp.concatenate(parts, axis=-1) + d_ref[...] * us_ref[rows, :]
        c0 = math.sqrt(2.0 / math.pi)
        return 0.5 * y * (1.0 + jnp.tanh(c0 * (y + 0.044715 * (y * y * y))))

    def glu(r0, zg):
        rows = slice(r0, r0 + part_rows)
        gl = _dot(zg.astype(_BF16), gw_ref[...]) + gb_ref[...]
        y_s = zg * jax.nn.sigmoid(gl)
        ycat_ref[rows, CONV_WIDTH:] = _rms(y_s, sn_ref[...]).astype(_BF16)

    def out_proj(r0):
        rows = slice(r0, r0 + part_rows)
        o_ref[rows, :] = x_ref[rows, :] + _dot(ycat_ref[rows, :], wo_ref[...])

    starts = [part * part_rows for part in range(MIX_PARTS)]
    us = [in_proj(r0) for r0 in starts]
    convert_next_weights()
    for r0, u_bf in zip(starts, us):
        state_in(r0, u_bf)
    zgs = []
    for r0 in starts:
        scan(r0)
        zgs.append(state_out(r0))
    for r0, zg in zip(starts, zgs):
        glu(r0, zg)
    for r0 in starts:
        out_proj(r0)
    vbuf_ref[0:halo, :] = vbuf_ref[ROW_TILE:ROW_TILE + halo, :]


def _mix(h, layer, stacked, w_bf, cast):
    (mix_norm, conv_w, conv_b, lr, li, wb, wc, ssm_d, glu_b, conv_norm, ssm_norm) = stacked
    w_in, glu_w, w_out = w_bf
    st = lambda a: _stacked(a.shape[1:], layer)
    wh = lambda a: _whole(a.shape)
    args = [h, mix_norm, w_in, conv_w, conv_b, lr, li, wb, wc, ssm_d, glu_w, glu_b,
            conv_norm, ssm_norm, w_out]
    in_specs = [_row_spec(D_MODEL, ROW_TILE), st(mix_norm), wh(w_in), st(conv_w), st(conv_b), st(lr),
                st(li), st(wb), st(wc), st(ssm_d), wh(glu_w), st(glu_b), st(conv_norm),
                st(ssm_norm), wh(w_out)]
    cast_w, cast_layer = cast
    c_in, c_out, c_shapes = _cast_plan(cast_w, cast_layer, ROWS // ROW_TILE)
    outs = pl.pallas_call(
        functools.partial(_mix_body, n_in=len(args), n_cast=len(cast_w)),
        grid=(ROWS // ROW_TILE,),
        in_specs=in_specs + c_in,
        out_specs=[_row_spec(D_MODEL, ROW_TILE)] + c_out,
        out_shape=[jax.ShapeDtypeStruct((ROWS, D_MODEL), _F32)] + c_shapes,
        scratch_shapes=[
            pltpu.VMEM((BATCH, STATE_COLS), _F32),
            pltpu.VMEM((ROW_TILE + (CONV_K - 1) * BATCH, CONV_WIDTH), _F32),
            pltpu.VMEM((ROW_TILE, SSM_WIDTH), _F32),
            pltpu.VMEM((ROW_TILE, STATE_COLS), _F32),
            pltpu.VMEM((ROW_TILE, STATE_COLS), _BF16),
            pltpu.VMEM((ROW_TILE, MIX_WIDTH), _BF16),
        ],
        compiler_params=_params(),
        name="mix",
    )(*args, *cast_w)
    return outs[0], tuple(outs[1:])


def _block_diag(w):
    d, _, a, b = w.shape
    rows, cols = GROUPS_PER_BLOCK * a, GROUPS_PER_BLOCK * b
    tiled = jnp.tile(w.reshape(d, N_BLOCKS, rows, b), (1, 1, 1, GROUPS_PER_BLOCK))
    row_group = lax.broadcasted_iota(jnp.int32, (rows, cols), 0) // a
    col_group = lax.broadcasted_iota(jnp.int32, (rows, cols), 1) // b
    return jnp.where(row_group == col_group, tiled, 0.0)


def _s5_params(A_re, A_im, B_re, B_im, C_re, C_im, log_dt):
    dt = jnp.exp(log_dt)[..., None]
    mag = jnp.exp(A_re * dt)
    ph = A_im * dt
    lb_re, lb_im = mag * jnp.cos(ph), mag * jnp.sin(ph)
    nr, ni = lb_re - 1.0, lb_im
    den = A_re * A_re + A_im * A_im
    f_re = (nr * A_re + ni * A_im) / den
    f_im = (ni * A_re - nr * A_im) / den
    bb_re = f_re[..., None] * B_re - f_im[..., None] * B_im
    bb_im = f_re[..., None] * B_im + f_im[..., None] * B_re
    wb = jnp.concatenate([_block_diag(jnp.swapaxes(bb_re, -1, -2)),
                          _block_diag(jnp.swapaxes(bb_im, -1, -2))], axis=-1)
    wc = jnp.concatenate([_block_diag(jnp.swapaxes(C_re, -1, -2)),
                          _block_diag(-jnp.swapaxes(C_im, -1, -2))], axis=-2)
    lam_shape = (DEPTH, BATCH, SSM_GROUPS * SSM_STATE)
    lr = jnp.broadcast_to(lb_re.reshape(DEPTH, 1, -1), lam_shape)
    li = jnp.broadcast_to(lb_im.reshape(DEPTH, 1, -1), lam_shape)
    return lr, li, wb.astype(_BF16), wc.astype(_BF16)


def kernel(x, p, ffn1_norm, ffn1_w_gate, ffn1_w_up, ffn1_w_down, mix_norm, w_in, conv_w, conv_b, ssm_A_re, ssm_A_im, ssm_B_re, ssm_B_im, ssm_C_re, ssm_C_im, ssm_D, ssm_log_dt, glu_w, glu_b, conv_out_norm, ssm_out_norm, w_out, ffn2_norm, ffn2_w_gate, ffn2_w_up, ffn2_w_down, ple_norm, ple_w_gate, ple_w_proj, final_norm):
    vec = lambda a: a.reshape(DEPTH, 1, -1)

    lr, li, wb, wc = _s5_params(ssm_A_re, ssm_A_im, ssm_B_re, ssm_B_im,
                                ssm_C_re, ssm_C_im, ssm_log_dt)
    mix_small = (vec(mix_norm), conv_w, vec(conv_b), lr, li, wb, wc, vec(ssm_D),
                 vec(glu_b), vec(conv_out_norm), vec(ssm_out_norm))
    ffn1_w = (ffn1_w_gate, ffn1_w_up, ffn1_w_down)
    mix_w = (w_in, glu_w, w_out)
    ffn2_w = (ffn2_w_gate, ffn2_w_up, ffn2_w_down, ple_w_gate, ple_w_proj)
    fin = final_norm.reshape(1, D_MODEL)

    f1_bf = tuple(w[0].astype(_BF16) for w in ffn1_w)
    h = x
    for i in range(DEPTH):
        h, mix_bf = _ffn(h, i, vec(ffn1_norm), f1_bf, row_tile=FFN_ROW_TILE,
                         x_batch_major=(i == 0), cast=(mix_w, i))
        h, f2_bf = _mix(h, i, mix_small, mix_bf, cast=(ffn2_w, i))
        last = i == DEPTH - 1
        h, f1_bf = _ffn(h, i, vec(ffn2_norm), f2_bf[:3], row_tile=ROW_TILE,
                        ple=(p, vec(ple_norm), f2_bf[3], f2_bf[4], fin), final=last,
                        cast=() if last else (ffn1_w, i + 1))
    return h
```
